```python
import jax, jax.numpy as jnp
from jax import lax
import numpy as np

D_MODEL = 2048
BATCH = 2
SEQ = 4096
DEPTH = 4
DEC_BATCH = 32
DEC_SEQ = 1
PAST_LEN = 16384
PAGE_SIZE = 128

N_EVEN = (DEPTH + 1) // 2
N_ODD = DEPTH // 2
A_HEADS = 8
A_KV_HEADS = 4
A_HEAD_DIM = 128
IDX_HEADS = 16
IDX_DIM = 64
TOPK = 256
A_QBLOCK = 128
B_HEADS = 4
B_DK = 128
B_DV = 256
B_GATE_RANK = 16
B_GATE_TAU = 16.0
B_CHUNK = 64
C_HEADS = 4
C_DK = 128
C_DV = 256
C_CHUNK = 64
D_HEADS = 16
D_KV_HEADS = 2
D_HEAD_DIM = 64
WINDOW = 128
D_FF = 5632
CONV_W = 3
EPS = 1e-6
NEG = -1e30
F32 = jnp.float32

EVEN_SIZES = (A_HEADS * A_HEAD_DIM, A_KV_HEADS * A_HEAD_DIM, A_KV_HEADS * A_HEAD_DIM,
              IDX_HEADS * IDX_DIM, IDX_DIM, IDX_HEADS,
              B_HEADS * B_DK, B_HEADS * B_DK, B_HEADS * B_DV, B_GATE_RANK, B_HEADS * B_DV)
EVEN_IN = sum(EVEN_SIZES)
EVEN_OUT = A_HEADS * A_HEAD_DIM + B_HEADS * B_DV
ODD_SIZES = (C_HEADS * C_DK, C_HEADS * C_DK, C_HEADS * C_DV, C_HEADS, C_HEADS, C_HEADS * C_DV,
             D_HEADS * D_HEAD_DIM, D_KV_HEADS * D_HEAD_DIM, D_KV_HEADS * D_HEAD_DIM)
ODD_IN = sum(ODD_SIZES)
ODD_OUT = C_HEADS * C_DV + D_HEADS * D_HEAD_DIM

kernel_name = 'hybrid_dsa_gla_mlstm_swa_convffn_step'


def _rms(x, g):
    xf = x.astype(F32)
    y = xf * lax.rsqrt(jnp.mean(xf * xf, axis=-1, keepdims=True) + EPS)
    return (y * g.astype(F32)).astype(x.dtype)


def _split(z, sizes):
    return jnp.split(z, np.cumsum(sizes)[:-1].tolist(), axis=-1)


def _chunks(z, chunk):
    b, t, h = z.shape[:3]
    z = z.reshape(b, t // chunk, chunk, h, *z.shape[3:])
    return jnp.moveaxis(z, (1, 3), (0, 2))


def _unchunk(z):
    z = jnp.moveaxis(z, (0, 2), (1, 3))
    return z.reshape(z.shape[0], z.shape[1] * z.shape[2], *z.shape[3:])


def _index_scores(qi, wi, ki):
    logits = jnp.einsum('bthd,bsd->bths', qi.astype(F32), ki.astype(F32)) * (IDX_DIM ** -0.5)
    return jnp.einsum('bth,bths->bts', wi.astype(F32), jax.nn.relu(logits)) * (IDX_HEADS ** -0.5)


def _sparse_attend(q, ksel, vsel, valid):
    b, t = q.shape[:2]
    qg = q.reshape(b, t, A_KV_HEADS, A_HEADS // A_KV_HEADS, A_HEAD_DIM).astype(F32)
    s = jnp.einsum('btgrd,btkgd->btgrk', qg, ksel.astype(F32)) * (A_HEAD_DIM ** -0.5)
    s = jnp.where(valid[:, :, None, None, :], s, NEG)
    p = jax.nn.softmax(s, axis=-1)
    o = jnp.einsum('btgrk,btkgd->btgrd', p, vsel.astype(F32))
    return o.reshape(b, t, A_HEADS, A_HEAD_DIM).astype(q.dtype)


def _dsa_prompt(q, k, v, qi, ki, wi):
    b, s = q.shape[:2]
    nb = s // A_QBLOCK
    topk = min(TOPK, s // 4)
    kpos = jnp.arange(s)

    def blocks(z):
        return z.reshape(b, nb, A_QBLOCK, *z.shape[2:]).swapaxes(0, 1)

    def one_block(inp):
        qb, qib, wb, start = inp
        qpos = start + jnp.arange(A_QBLOCK)
        score = _index_scores(qib, wb, ki)
        score = jnp.where(kpos[None, :] <= qpos[:, None], score, NEG)
        val, idx = lax.top_k(score, topk)
        ksel = jax.vmap(lambda kk, ii: kk[ii])(k, idx)
        vsel = jax.vmap(lambda vv, ii: vv[ii])(v, idx)
        return _sparse_attend(qb, ksel, vsel, val > NEG * 0.5)

    out = lax.map(one_block, (blocks(q), blocks(qi), blocks(wi), jnp.arange(nb) * A_QBLOCK))
    return out.swapaxes(0, 1).reshape(b, s, A_HEADS, A_HEAD_DIM)


def _dsa_sample(q, k_new, v_new, qi, ki_new, wi, pool_k, pool_v, pool_ki, page_table):
    bd, t = q.shape[:2]
    past = page_table.shape[1] * PAGE_SIZE
    total = past + t
    topk = min(TOPK, total // 4)
    ki_past = pool_ki[page_table].reshape(bd, past, IDX_DIM)
    ki_all = jnp.concatenate([ki_past.astype(ki_new.dtype), ki_new], axis=1)
    score = _index_scores(qi, wi, ki_all)
    allowed = jnp.arange(total)[None, :] <= (past + jnp.arange(t))[:, None]
    score = jnp.where(allowed, score, NEG)
    val, idx = lax.top_k(score, topk)
    bidx = jnp.arange(bd)[:, None, None]
    ip = jnp.minimum(idx, past - 1)
    phys = page_table[bidx, ip // PAGE_SIZE]
    off = ip % PAGE_SIZE
    inew = jnp.clip(idx - past, 0, t - 1)
    in_past = (idx < past)[..., None, None]
    ksel = jnp.where(in_past, pool_k[phys, off].astype(k_new.dtype), k_new[bidx, inew])
    vsel = jnp.where(in_past, pool_v[phys, off].astype(v_new.dtype), v_new[bidx, inew])
    return _sparse_attend(q, ksel, vsel, val > NEG * 0.5)


def _gla(q, k, v, log_a, s0):
    t = q.shape[1]
    chunk = B_CHUNK if t % B_CHUNK == 0 else t
    qc = _chunks(q.astype(F32) * (B_DK ** -0.5), chunk)
    kc = _chunks(k.astype(F32), chunk)
    vc = _chunks(v.astype(F32), chunk)
    gc = _chunks(log_a.astype(F32), chunk)
    mask = jnp.tril(jnp.ones((chunk, chunk), dtype=bool))

    def step(s, inp):
        qb, kb, vb, gb = inp
        bc = jnp.cumsum(gb, axis=2)
        bl = bc[:, :, -1:, :]
        qd = qb * jnp.exp(bc)
        att = jnp.where(mask, jnp.einsum('bhid,bhjd->bhij', qd, kb * jnp.exp(-bc)), 0.0)
        o = jnp.einsum('bhid,bhdv->bhiv', qd, s) + jnp.einsum('bhij,bhjv->bhiv', att, vb)
        s = jnp.exp(bl[:, :, 0, :, None]) * s + jnp.einsum('bhjd,bhjv->bhdv', kb * jnp.exp(bl - bc), vb)
        return s, o

    s_fin, o = lax.scan(step, s0.astype(F32), (qc, kc, vc, gc))
    return _unchunk(o), s_fin


def _mlstm(q, k, v, i_pre, log_f, c0, n0, m0):
    t = q.shape[1]
    chunk = C_CHUNK if t % C_CHUNK == 0 else t
    qc = _chunks(q.astype(F32), chunk)
    kc = _chunks(k.astype(F32) * (C_DK ** -0.5), chunk)
    vc = _chunks(v.astype(F32), chunk)
    ic = _chunks(i_pre.astype(F32), chunk)
    fc = _chunks(log_f.astype(F32), chunk)
    mask = jnp.tril(jnp.ones((chunk, chunk), dtype=bool))

    def step(carry, inp):
        cm, nv, m = carry
        qb, kb, vb, ib, fb = inp
        b = jnp.cumsum(fb, axis=-1)
        log_d = jnp.where(mask, b[..., :, None] - b[..., None, :] + ib[..., None, :], NEG)
        inter = b + m[..., None]
        m_t = jnp.maximum(inter, jnp.max(log_d, axis=-1))
        dmat = jnp.exp(log_d - m_t[..., None])
        g = jnp.exp(inter - m_t)
        qk = jnp.einsum('bhtd,bhjd->bhtj', qb, kb) * dmat
        num = g[..., None] * jnp.einsum('bhtd,bhdv->bhtv', qb, cm) + jnp.einsum('bhtj,bhjv->bhtv', qk, vb)
        den = g * jnp.einsum('bhtd,bhd->bht', qb, nv) + jnp.sum(qk, axis=-1)
        h = num / jnp.maximum(jnp.abs(den), jnp.exp(-m_t))[..., None]
        w_last = dmat[..., -1, :]
        g_last = g[..., -1]
        cm = g_last[..., None, None] * cm + jnp.einsum('bhj,bhjd,bhjv->bhdv', w_last, kb, vb)
        nv = g_last[..., None] * nv + jnp.einsum('bhj,bhjd->bhd', w_last, kb)
        return (cm, nv, m_t[..., -1]), h

    (cf, nf, mf), h = lax.scan(step, (c0.astype(F32), n0.astype(F32), m0.astype(F32)), (qc, kc, vc, ic, fc))
    return _unchunk(h), cf, nf, mf


def _sink_attend(q, k, v, allowed, sinks):
    s = jnp.einsum('...qgrd,...kgd->...grqk', q.astype(F32), k.astype(F32)) * (D_HEAD_DIM ** -0.5)
    s = jnp.where(allowed, s, NEG)
    sk = sinks.astype(F32).reshape(D_KV_HEADS, D_HEADS // D_KV_HEADS)[:, :, None]
    m = jnp.maximum(jnp.max(s, axis=-1), sk)
    e = jnp.exp(s - m[..., None])
    p = e / (jnp.sum(e, axis=-1) + jnp.exp(sk - m))[..., None]
    return jnp.einsum('...grqk,...kgd->...qgrd', p, v.astype(F32)).astype(q.dtype)


def _swa_prompt(q, k, v, sinks):
    b, s = q.shape[:2]
    nb = s // WINDOW
    qb = q.reshape(b, nb, WINDOW, *q.shape[2:])

    def band(z):
        zb = z.reshape(b, nb, WINDOW, *z.shape[2:])
        prev = jnp.pad(zb, ((0, 0), (1, 0)) + ((0, 0),) * (zb.ndim - 2))[:, :-1]
        return jnp.concatenate([prev, zb], axis=2)

    qpos = WINDOW + jnp.arange(WINDOW)
    kpos = jnp.arange(2 * WINDOW)
    diff = qpos[:, None] - kpos[None, :]
    allowed = (diff >= 0) & (diff <= WINDOW)
    allowed = allowed[None] & ((jnp.arange(nb) > 0)[:, None, None] | (kpos >= WINDOW)[None, None, :])
    o = _sink_attend(qb, band(k), band(v), allowed[None, :, None, None], sinks)
    return o.reshape(b, s, D_HEADS * D_HEAD_DIM)


def _swa_sample(q, k_new, v_new, buf_k, buf_v, sinks):
    bd, t = q.shape[:2]
    kk = jnp.concatenate([buf_k.astype(k_new.dtype), k_new], axis=1)
    vv = jnp.concatenate([buf_v.astype(v_new.dtype), v_new], axis=1)
    diff = (WINDOW + jnp.arange(t))[:, None] - jnp.arange(WINDOW + t)[None, :]
    allowed = (diff >= 0) & (diff <= WINDOW)
    o = _sink_attend(q, kk, vv, allowed, sinks)
    return o.reshape(bd, t, D_HEADS * D_HEAD_DIM), kk[:, -WINDOW:], vv[:, -WINDOW:]


def _even_in(h, w_in, a_qn, a_kn, a_kin, b_w2, b_b2):
    b, t = h.shape[:2]
    qa, ka, va, qi, ki, wi, qb, kb, vb, ga, rb = _split(h @ w_in, EVEN_SIZES)
    qa = _rms(qa.reshape(b, t, A_HEADS, A_HEAD_DIM), a_qn)
    ka = _rms(ka.reshape(b, t, A_KV_HEADS, A_HEAD_DIM), a_kn)
    va = va.reshape(b, t, A_KV_HEADS, A_HEAD_DIM)
    qi = qi.reshape(b, t, IDX_HEADS, IDX_DIM)
    ki = _rms(ki, a_kin)
    log_a = jax.nn.log_sigmoid((ga @ b_w2 + b_b2).astype(F32)) / B_GATE_TAU
    gla = (qb.reshape(b, t, B_HEADS, B_DK), kb.reshape(b, t, B_HEADS, B_DK),
           vb.reshape(b, t, B_HEADS, B_DV), log_a.reshape(b, t, B_HEADS, B_DK))
    return (qa, ka, va, qi, ki, wi), gla, rb


def _even_out(a_o, g_o, rb, b_gn, w_out):
    b, t = a_o.shape[:2]
    g_o = _rms(g_o.astype(rb.dtype), b_gn).reshape(b, t, -1) * jax.nn.silu(rb)
    return jnp.concatenate([a_o.reshape(b, t, -1), g_o], axis=-1) @ w_out


def _odd_in(h, w_in, c_bi, c_bf, d_qn, d_kn):
    b, t = h.shape[:2]
    qc, kc, vc, ic, fc, oc, qd, kd, vd = _split(h @ w_in, ODD_SIZES)
    mlstm = (qc.reshape(b, t, C_HEADS, C_DK), kc.reshape(b, t, C_HEADS, C_DK),
             vc.reshape(b, t, C_HEADS, C_DV), (ic + c_bi).astype(F32),
             jax.nn.log_sigmoid((fc + c_bf).astype(F32)))
    qd = _rms(qd.reshape(b, t, D_HEADS, D_HEAD_DIM), d_qn).reshape(
        b, t, D_KV_HEADS, D_HEADS // D_KV_HEADS, D_HEAD_DIM)
    kd = _rms(kd.reshape(b, t, D_KV_HEADS, D_HEAD_DIM), d_kn)
    vd = vd.reshape(b, t, D_KV_HEADS, D_HEAD_DIM)
    return mlstm, oc, (qd, kd, vd)


def _odd_out(c_h, oc, c_hn, d_o, w_out):
    b, t = oc.shape[:2]
    c_h = _rms(c_h.astype(oc.dtype), c_hn).reshape(b, t, -1) * jax.nn.sigmoid(oc)
    return jnp.concatenate([c_h, d_o.reshape(b, t, -1)], axis=-1) @ w_out


def _ffn(h, w_up, conv_w, conv_b, w_down, buf):
    t = h.shape[1]
    u, g = jnp.split(h @ w_up, 2, axis=-1)
    g_ext = jnp.concatenate([buf.astype(g.dtype), g], axis=1)
    gc = conv_b
    for j in range(CONV_W):
        gc = gc + conv_w[j] * g_ext[:, j:j + t]
    return (jax.nn.silu(gc) * u) @ w_down, g_ext[:, t:]


def setup_inputs(seed: int = 0) -> dict:
    key = jax.random.key(seed)
    keys = list(jax.random.split(key, 40))

    def nrm(shape, scale):
        return jax.random.normal(keys.pop(), shape, F32) * scale

    def gain(shape):
        return 1.0 + nrm(shape, 0.05)

    n_pages = PAST_LEN // PAGE_SIZE
    n_used = DEC_BATCH * n_pages
    n_phys = (5 * n_used) // 4
    perm = jax.random.permutation(keys.pop(), n_phys)
    page_table = perm[:n_used].reshape(DEC_BATCH, n_pages).astype(jnp.int32)
    return {
        'x_prompt': nrm((BATCH, SEQ, D_MODEL), 1.0),
        'x_sample': nrm((DEC_BATCH, DEC_SEQ, D_MODEL), 1.0),
        'cache_a_k': nrm((N_EVEN, n_phys, PAGE_SIZE, A_KV_HEADS, A_HEAD_DIM), 1.0),
        'cache_a_v': nrm((N_EVEN, n_phys, PAGE_SIZE, A_KV_HEADS, A_HEAD_DIM), 1.0),
        'cache_a_kidx': nrm((N_EVEN, n_phys, PAGE_SIZE, IDX_DIM), 1.0),
        'state_b_s': nrm((N_EVEN, DEC_BATCH, B_HEADS, B_DK, B_DV), 1.0),
        'state_c_c': nrm((N_ODD, DEC_BATCH, C_HEADS, C_DK, C_DV), 1.0),
        'state_c_n': nrm((N_ODD, DEC_BATCH, C_HEADS, C_DK), 1.0),
        'state_c_m': nrm((N_ODD, DEC_BATCH, C_HEADS), 1.0),
        'cache_d_k': nrm((N_ODD, DEC_BATCH, WINDOW, D_KV_HEADS, D_HEAD_DIM), 1.0),
        'cache_d_v': nrm((N_ODD, DEC_BATCH, WINDOW, D_KV_HEADS, D_HEAD_DIM), 1.0),
        'state_ffn_conv': nrm((DEPTH, DEC_BATCH, CONV_W - 1, D_FF), 1.0),
        'page_table': page_table,
        'norm_mix': gain((DEPTH, D_MODEL)),
        'norm_ffn': gain((DEPTH, D_MODEL)),
        'even_w_in': nrm((N_EVEN, D_MODEL, EVEN_IN), D_MODEL ** -0.5),
        'even_w_out': nrm((N_EVEN, EVEN_OUT, D_MODEL), EVEN_OUT ** -0.5),
        'a_q_norm': gain((N_EVEN, A_HEAD_DIM)),
        'a_k_norm': gain((N_EVEN, A_HEAD_DIM)),
        'a_kidx_norm': gain((N_EVEN, IDX_DIM)),
        'b_gate_w2': nrm((N_EVEN, B_GATE_RANK, B_HEADS * B_DK), B_GATE_RANK ** -0.5),
        'b_gate_b': nrm((N_EVEN, B_HEADS * B_DK), 0.1),
        'b_head_norm': gain((N_EVEN, B_DV)),
        'odd_w_in': nrm((N_ODD, D_MODEL, ODD_IN), D_MODEL ** -0.5),
        'odd_w_out': nrm((N_ODD, ODD_OUT, D_MODEL), ODD_OUT ** -0.5),
        'c_i_bias': nrm((N_ODD, C_HEADS), 0.1),
        'c_f_bias': 3.0 + nrm((N_ODD, C_HEADS), 0.1),
        'c_head_norm': gain((N_ODD, C_DV)),
        'd_q_norm': gain((N_ODD, D_HEAD_DIM)),
        'd_k_norm': gain((N_ODD, D_HEAD_DIM)),
        'd_sinks': nrm((N_ODD, D_HEADS), 0.5),
        'ffn_w_up': nrm((DEPTH, D_MODEL, 2 * D_FF), D_MODEL ** -0.5),
        'ffn_conv_w': nrm((DEPTH, CONV_W, D_FF), CONV_W ** -0.5),
        'ffn_conv_b': nrm((DEPTH, D_FF), 0.02),
        'ffn_w_down': nrm((DEPTH, D_FF, D_MODEL), D_FF ** -0.5),
    }


def reference(x_prompt, x_sample, cache_a_k, cache_a_v, cache_a_kidx, state_b_s, state_c_c, state_c_n,
              state_c_m, cache_d_k, cache_d_v, state_ffn_conv, page_table, norm_mix, norm_ffn,
              even_w_in, even_w_out, a_q_norm, a_k_norm, a_kidx_norm, b_gate_w2, b_gate_b, b_head_norm,
              odd_w_in, odd_w_out, c_i_bias, c_f_bias, c_head_norm, d_q_norm, d_k_norm, d_sinks,
              ffn_w_up, ffn_conv_w, ffn_conv_b, ffn_w_down):
    xp, xs = x_prompt, x_sample
    bp = xp.shape[0]
    dt = xp.dtype
    p_ak, p_av, p_aki, p_bs, p_cc, p_cn, p_cm, p_dk, p_dv, p_conv = [], [], [], [], [], [], [], [], [], []
    s_ak, s_av, s_aki, s_bs, s_cc, s_cn, s_cm, s_dk, s_dv, s_conv = [], [], [], [], [], [], [], [], [], []
    for layer in range(DEPTH):
        hp = _rms(xp, norm_mix[layer])
        hs = _rms(xs, norm_mix[layer])
        if layer % 2 == 0:
            e = layer // 2
            wts = (even_w_in[e], a_q_norm[e], a_k_norm[e], a_kidx_norm[e], b_gate_w2[e], b_gate_b[e])
            (qa, ka, va, qi, ki, wi), gla, rb = _even_in(hp, *wts)
            a_o = _dsa_prompt(qa, ka, va, qi, ki, wi)
            g_o, s_fin = _gla(*gla, jnp.zeros((bp, B_HEADS, B_DK, B_DV), F32))
            mp = _even_out(a_o, g_o, rb, b_head_norm[e], even_w_out[e])
            p_ak.append(ka); p_av.append(va); p_aki.append(ki); p_bs.append(s_fin.astype(dt))
            (qa, ka, va, qi, ki, wi), gla, rb = _even_in(hs, *wts)
            a_o = _dsa_sample(qa, ka, va, qi, ki, wi, cache_a_k[e], cache_a_v[e], cache_a_kidx[e], page_table)
            g_o, s_fin = _gla(*gla, state_b_s[e])
            ms = _even_out(a_o, g_o, rb, b_head_norm[e], even_w_out[e])
            s_ak.append(ka); s_av.append(va); s_aki.append(ki); s_bs.append(s_fin.astype(dt))
        else:
            o = layer // 2
            wts = (odd_w_in[o], c_i_bias[o], c_f_bias[o], d_q_norm[o], d_k_norm[o])
            ml, oc, (qd, kd, vd) = _odd_in(hp, *wts)
            c_h, cf, nf, mf = _mlstm(*ml, jnp.zeros((bp, C_HEADS, C_DK, C_DV), F32),
                                     jnp.zeros((bp, C_HEADS, C_DK), F32), jnp.zeros((bp, C_HEADS), F32))
            d_o = _swa_prompt(qd, kd, vd, d_sinks[o])
            mp = _odd_out(c_h, oc, c_head_norm[o], d_o, odd_w_out[o])
            p_cc.append(cf.astype(dt)); p_cn.append(nf.astype(dt)); p_cm.append(mf.astype(dt))
            p_dk.append(kd[:, -WINDOW:]); p_dv.append(vd[:, -WINDOW:])
            ml, oc, (qd, kd, vd) = _odd_in(hs, *wts)
            c_h, cf, nf, mf = _mlstm(*ml, state_c_c[o], state_c_n[o], state_c_m[o])
            d_o, nbk, nbv = _swa_sample(qd, kd, vd, cache_d_k[o], cache_d_v[o], d_sinks[o])
            ms = _odd_out(c_h, oc, c_head_norm[o], d_o, odd_w_out[o])
            s_cc.append(cf.astype(dt)); s_cn.append(nf.astype(dt)); s_cm.append(mf.astype(dt))
            s_dk.append(nbk); s_dv.append(nbv)
        xp = xp + mp
        xs = xs + ms
        fw = (ffn_w_up[layer], ffn_conv_w[layer], ffn_conv_b[layer], ffn_w_down[layer])
        fp, cp = _ffn(_rms(xp, norm_ffn[layer]), *fw, jnp.zeros((bp, CONV_W - 1, D_FF), dt))
        fs, cs = _ffn(_rms(xs, norm_ffn[layer]), *fw, state_ffn_conv[layer])
        xp = xp + fp
        xs = xs + fs
        p_conv.append(cp)
        s_conv.append(cs)
    return (xp, xs,
            jnp.stack(p_ak), jnp.stack(p_av), jnp.stack(p_aki), jnp.stack(p_bs),
            jnp.stack(p_cc), jnp.stack(p_cn), jnp.stack(p_cm), jnp.stack(p_dk), jnp.stack(p_dv), jnp.stack(p_conv),
            jnp.stack(s_ak), jnp.stack(s_av), jnp.stack(s_aki), jnp.stack(s_bs),
            jnp.stack(s_cc), jnp.stack(s_cn), jnp.stack(s_cm), jnp.stack(s_dk), jnp.stack(s_dv), jnp.stack(s_conv))
```

```python
import functools

import jax
import jax.numpy as jnp
from jax import lax
from jax.experimental import pallas as pl
from jax.experimental.pallas import tpu as pltpu

F32 = jnp.float32
BF16 = jnp.bfloat16
I32 = jnp.int32

EPS = 1e-6
NEG = -1e30

PAGE = 128
A_HEADS, A_KV, A_DH = 8, 4, 128
IDX_HEADS, IDX_DIM = 16, 64
TOPK = 256
B_HEADS, B_DK, B_DV, B_RANK, B_TAU = 4, 128, 256, 16, 16.0
C_HEADS, C_DK, C_DV = 4, 128, 256
D_HEADS, D_KV, D_DH = 16, 2, 64
WINDOW = 128
CONV_W = 3

VMEM_LIMIT = 56 * 1024 * 1024
LANES = 128
SUBLANES = 8


def _cparams(sem):
    return pltpu.CompilerParams(dimension_semantics=sem, vmem_limit_bytes=VMEM_LIMIT)


def _dot(a, b):
    return jnp.dot(a, b, preferred_element_type=F32)


def _dot_nt(a, b):
    return lax.dot_general(a, b, (((1,), (1,)), ((), ())), preferred_element_type=F32)


def _split3(x):
    hi = x.astype(BF16)
    r1 = x - hi.astype(F32)
    mid = r1.astype(BF16)
    lo = (r1 - mid.astype(F32)).astype(BF16)
    return hi, mid, lo


def _dot_exact_lhs(a_bf16, x):
    hi, mid, lo = _split3(x)
    return _dot(a_bf16, hi) + _dot(a_bf16, mid) + _dot(a_bf16, lo)


def _log_sigmoid(x):
    return jnp.minimum(x, 0.0) - jnp.log(1.0 + jnp.exp(-jnp.abs(x)))


def _sigmoid(x):
    return 1.0 / (1.0 + jnp.exp(-x))


def _silu(x):
    return x * _sigmoid(x)


def _iota(shape, dim):
    return lax.broadcasted_iota(I32, shape, dim)


def _rms_cast_kernel(x_ref, g_ref, o_ref):
    x = x_ref[...]
    ms = jnp.mean(x * x, axis=-1, keepdims=True)
    o_ref[...] = (x * lax.rsqrt(ms + EPS) * g_ref[...]).astype(o_ref.dtype)


def _rms_cast(x, gain, tm):
    m, d = x.shape
    return pl.pallas_call(
        _rms_cast_kernel,
        out_shape=jax.ShapeDtypeStruct((m, d), BF16),
        grid=(m // tm,),
        in_specs=[pl.BlockSpec((tm, d), lambda i: (i, 0)),
                  pl.BlockSpec((1, d), lambda i: (0, 0))],
        out_specs=pl.BlockSpec((tm, d), lambda i: (i, 0)),
        compiler_params=_cparams(("arbitrary",)),
        name="rms_cast",
    )(x, gain.reshape(1, d))


def _group_rms(acc, gain, width):
    tn = acc.shape[1]
    outs = []
    for c in range(tn // LANES):
        blk = acc[:, c * LANES:(c + 1) * LANES]
        sq = blk * blk
        if width == LANES:
            ms = jnp.sum(sq, axis=-1, keepdims=True) * (1.0 / width)
        else:
            low = _iota(blk.shape, 1) < width
            s_lo = jnp.sum(jnp.where(low, sq, 0.0), axis=-1, keepdims=True)
            s_hi = jnp.sum(jnp.where(low, 0.0, sq), axis=-1, keepdims=True)
            ms = jnp.where(low, s_lo, s_hi) * (1.0 / width)
        outs.append(blk * lax.rsqrt(ms + EPS) * gain[:, c * LANES:(c + 1) * LANES])
    return outs[0] if len(outs) == 1 else jnp.concatenate(outs, axis=1)


def _proj_kernel(h_ref, w_ref, *rest, mode, n_out):
    acc = _dot(h_ref[...], w_ref[...])
    outs = rest[-n_out:]
    if mode == "raw":
        y = acc
    elif mode == "rms128":
        y = _group_rms(acc, rest[0][...], 128)
    elif mode == "rms64":
        y = _group_rms(acc, rest[0][...], 64)
    elif mode == "kd_vd":
        y = jnp.concatenate([_group_rms(acc[:, :LANES], rest[0][:, :LANES], 64), acc[:, LANES:]], axis=1)
    elif mode == "small_even":
        low = _iota(acc.shape, 1) < IDX_DIM
        ms = jnp.sum(jnp.where(low, acc * acc, 0.0), axis=-1, keepdims=True) * (1.0 / IDX_DIM)
        kin = acc * lax.rsqrt(ms + EPS) * rest[0][...]
        y = jnp.where(low, kin, acc)
        ka = jnp.where(low, kin, 0.0)
        outs[1][...] = ka.astype(BF16)
        outs[2][...] = pltpu.roll(ka, IDX_DIM, 1).astype(BF16)
        outs[0][...] = y
        return
    else:
        raise ValueError(mode)
    for o in outs:
        o[...] = y.astype(o.dtype)


def _proj(h, w, mode, out_dtypes, gain=None, tm=1024, tn=512):
    m, k = h.shape
    n = w.shape[1]
    tm = min(tm, m)
    tn = min(tn, n)
    in_specs = [pl.BlockSpec((tm, k), lambda i, j: (i, 0)),
                pl.BlockSpec((k, tn), lambda i, j: (0, j))]
    args = [h, w]
    if gain is not None:
        in_specs.append(pl.BlockSpec((1, tn), lambda i, j: (0, j)))
        args.append(gain.reshape(1, n).astype(F32))
    outs = pl.pallas_call(
        functools.partial(_proj_kernel, mode=mode, n_out=len(out_dtypes)),
        out_shape=[jax.ShapeDtypeStruct((m, n), dt) for dt in out_dtypes],
        grid=(m // tm, n // tn),
        in_specs=in_specs,
        out_specs=[pl.BlockSpec((tm, tn), lambda i, j: (i, j)) for _ in out_dtypes],
        compiler_params=_cparams(("arbitrary", "arbitrary")),
        name="proj_" + mode,
    )(*args)
    return outs


def _outproj_kernel(a_ref, b_ref, wa_ref, wb_ref, x_ref, g_ref, xo_ref, ho_ref):
    y = x_ref[...] + _dot(a_ref[...], wa_ref[...]) + _dot(b_ref[...], wb_ref[...])
    xo_ref[...] = y
    ms = jnp.mean(y * y, axis=-1, keepdims=True)
    ho_ref[...] = (y * lax.rsqrt(ms + EPS) * g_ref[...]).astype(ho_ref.dtype)


def _outproj(a, b, w, x, gain, tm=512):
    m, d = x.shape
    ka, kb = a.shape[1], b.shape[1]
    tm = min(tm, m)
    return pl.pallas_call(
        _outproj_kernel,
        out_shape=[jax.ShapeDtypeStruct((m, d), F32), jax.ShapeDtypeStruct((m, d), BF16)],
        grid=(m // tm,),
        in_specs=[pl.BlockSpec((tm, ka), lambda i: (i, 0)),
                  pl.BlockSpec((tm, kb), lambda i: (i, 0)),
                  pl.BlockSpec((ka, d), lambda i: (0, 0)),
                  pl.BlockSpec((kb, d), lambda i: (0, 0)),
                  pl.BlockSpec((tm, d), lambda i: (i, 0)),
                  pl.BlockSpec((1, d), lambda i: (0, 0))],
        out_specs=[pl.BlockSpec((tm, d), lambda i: (i, 0)),
                   pl.BlockSpec((tm, d), lambda i: (i, 0))],
        compiler_params=_cparams(("arbitrary",)),
        name="outproj",
    )(a, b, w[:ka], w[ka:], x, gain.reshape(1, d))


def _ffn_kernel(x_ref, h_ref, wu_ref, wg_ref, cw_ref, cb_ref, wd_ref, gn_ref, st_ref,
                xo_ref, ho_ref, gl_ref, acc_ref, gbuf_ref, carry_ref, *, tiles_per_seq, stepwise):
    i = pl.program_id(0)
    f = pl.program_id(1)
    nf = pl.num_programs(1)
    tm = h_ref.shape[0]
    h = h_ref[...]
    u = _dot(h, wu_ref[...])
    g = _dot(h, wg_ref[...])
    cw = cw_ref[...]
    if stepwise:
        gm2 = st_ref[0]
        gm1 = st_ref[1]
        gl_ref[...] = g
    else:
        first = (i % tiles_per_seq) == 0

        @pl.when(first)
        def _():
            gbuf_ref[0:SUBLANES, :] = st_ref[0]

        @pl.when(jnp.logical_not(first))
        def _():
            gbuf_ref[0:SUBLANES, :] = carry_ref[f]

        gbuf_ref[SUBLANES:SUBLANES + tm, :] = g
        gm1 = gbuf_ref[SUBLANES - 1:SUBLANES - 1 + tm, :]
        gm2 = gbuf_ref[SUBLANES - 2:SUBLANES - 2 + tm, :]
        tail = g[tm - SUBLANES:, :]
        carry_ref[f] = tail
        gl_ref[0] = tail
    gc = cb_ref[...] + cw[0:1, :] * gm2 + cw[1:2, :] * gm1 + cw[2:3, :] * g
    a = (_silu(gc) * u).astype(BF16)
    part = _dot(a, wd_ref[...])

    @pl.when(f == 0)
    def _():
        acc_ref[...] = x_ref[...] + part

    @pl.when(f > 0)
    def _():
        acc_ref[...] += part

    @pl.when(f == nf - 1)
    def _():
        y = acc_ref[...]
        xo_ref[...] = y
        ms = jnp.mean(y * y, axis=-1, keepdims=True)
        ho_ref[...] = (y * lax.rsqrt(ms + EPS) * gn_ref[...]).astype(ho_ref.dtype)


def _ffn(x, h, wu, wg, cw, cb, wd, gain_next, state, *, seq_len, stepwise, tm=512, tf=512):
    m, d = x.shape
    ff = wu.shape[1]
    tm = min(tm, m)
    n_i, n_f = m // tm, ff // tf
    tiles_per_seq = max(seq_len // tm, 1)
    if stepwise:
        st_spec = pl.BlockSpec((2, tm, tf), lambda i, f: (0, i, f))
        gl_shape = jax.ShapeDtypeStruct((m, ff), F32)
        gl_spec = pl.BlockSpec((tm, tf), lambda i, f: (i, f))
    else:
        st_spec = pl.BlockSpec((1, SUBLANES, tf), lambda i, f: (i // tiles_per_seq, 0, f))
        gl_shape = jax.ShapeDtypeStruct((n_i, SUBLANES, ff), F32)
        gl_spec = pl.BlockSpec((1, SUBLANES, tf), lambda i, f: (i, 0, f))
    return pl.pallas_call(
        functools.partial(_ffn_kernel, tiles_per_seq=tiles_per_seq, stepwise=stepwise),
        out_shape=[jax.ShapeDtypeStruct((m, d), F32), jax.ShapeDtypeStruct((m, d), BF16), gl_shape],
        grid=(n_i, n_f),
        in_specs=[pl.BlockSpec((tm, d), lambda i, f: (i, 0)),
                  pl.BlockSpec((tm, d), lambda i, f: (i, 0)),
                  pl.BlockSpec((d, tf), lambda i, f: (0, f)),
                  pl.BlockSpec((d, tf), lambda i, f: (0, f)),
                  pl.BlockSpec((CONV_W, tf), lambda i, f: (0, f)),
                  pl.BlockSpec((1, tf), lambda i, f: (0, f)),
                  pl.BlockSpec((tf, d), lambda i, f: (f, 0)),
                  pl.BlockSpec((1, d), lambda i, f: (0, 0)),
                  st_spec],
        out_specs=[pl.BlockSpec((tm, d), lambda i, f: (i, 0)),
                   pl.BlockSpec((tm, d), lambda i, f: (i, 0)),
                   gl_spec],
        scratch_shapes=[pltpu.VMEM((tm, d), F32),
                        pltpu.VMEM((SUBLANES + tm, tf), F32),
                        pltpu.VMEM((n_f, SUBLANES, tf), F32)],
        compiler_params=_cparams(("arbitrary", "arbitrary")),
        name="ffn_step" if stepwise else "ffn",
    )(x, h, wu, wg, cw, cb.reshape(1, ff), wd, gain_next.reshape(1, d), state)


def _bisect_threshold(count_ge, lo, hi, k, max_iter=80):
    def cond(c):
        it, _, _, done = c
        return jnp.logical_and(it < max_iter, jnp.min(done) == 0)

    def body(c):
        it, lo, hi, done = c
        mid = 0.5 * (lo + hi)
        cnt = count_ge(mid)
        ge = cnt >= k
        stuck = jnp.logical_or(mid <= lo, mid >= hi)
        new_done = jnp.logical_or(jnp.logical_and(ge, cnt == k), stuck)
        lo = jnp.where(ge, mid, lo)
        hi = jnp.where(ge, hi, mid)
        return it + 1, lo, hi, jnp.maximum(done, new_done.astype(I32))

    done0 = (count_ge(lo) == k).astype(I32)
    _, lo, _, _ = lax.while_loop(cond, body, (jnp.int32(0), lo, hi, done0))
    return lo


def _dsa_prompt_kernel(q_ref, k_ref, v_ref, qi_ref, kia_ref, kib_ref, sm_ref, o_ref, sc_ref,
                       *, tq, ts, topk):
    i = pl.program_id(1)
    q0 = i * tq
    nk = (q0 + tq + ts - 1) // ts
    qpos = q0 + _iota((tq, 1), 0)
    sm = sm_ref[...]
    wcols = [sm[:, IDX_DIM + h:IDX_DIM + h + 1] for h in range(IDX_HEADS)]

    def score_tile(kt, carry):
        ks = pl.multiple_of(kt * ts, ts)
        ka = kia_ref[pl.ds(ks, ts), :]
        kb = kib_ref[pl.ds(ks, ts), :]
        acc = jnp.zeros((tq, ts), F32)
        for p in range(IDX_HEADS // 2):
            qp = qi_ref[:, p * LANES:(p + 1) * LANES]
            acc = acc + wcols[2 * p] * jnp.maximum(_dot_nt(qp, ka), 0.0)
            acc = acc + wcols[2 * p + 1] * jnp.maximum(_dot_nt(qp, kb), 0.0)
        kpos = ks + _iota((1, ts), 1)
        sc_ref[kt] = jnp.where(kpos <= qpos, acc * (IDX_HEADS ** -0.5), NEG)
        return carry

    lax.fori_loop(0, nk, score_tile, 0)

    def reduce_tiles(fn, init, comb):
        def body(kt, c):
            return comb(c, fn(sc_ref[kt]))
        return lax.fori_loop(0, nk, body, init)

    def count_ge(t):
        return reduce_tiles(lambda s: jnp.sum((s >= t).astype(I32), axis=1, keepdims=True),
                            jnp.zeros((tq, 1), I32), lambda a, b: a + b)

    hi = reduce_tiles(lambda s: jnp.max(s, axis=1, keepdims=True),
                      jnp.full((tq, 1), NEG, F32), jnp.maximum)
    lo = reduce_tiles(lambda s: jnp.min(jnp.where(s > 0.5 * NEG, s, -NEG), axis=1, keepdims=True),
                      jnp.full((tq, 1), -NEG, F32), jnp.minimum)
    thr = _bisect_threshold(count_ge, lo, hi, topk)
    thr = jnp.where(qpos + 1 <= topk, NEG, thr)

    scale = A_DH ** -0.5
    rep = A_HEADS // A_KV
    for g in range(A_KV):
        qg = jnp.concatenate(
            [q_ref[:, (g * rep + r) * A_DH:(g * rep + r + 1) * A_DH] for r in range(rep)], axis=0)

        def attend(kt, c, g=g, qg=qg):
            m, l, acc = c
            ks = pl.multiple_of(kt * ts, ts)
            kk = k_ref[pl.ds(ks, ts), g * A_DH:(g + 1) * A_DH]
            vv = v_ref[pl.ds(ks, ts), g * A_DH:(g + 1) * A_DH]
            s = _dot_nt(qg, kk) * scale
            sc = sc_ref[kt]
            keep = jnp.logical_and(sc >= thr, sc > 0.5 * NEG)
            keep = jnp.concatenate([keep] * rep, axis=0)
            m_new = jnp.maximum(m, jnp.max(jnp.where(keep, s, NEG), axis=1, keepdims=True))
            p = jnp.where(keep, jnp.exp(s - m_new), 0.0)
            alpha = jnp.exp(m - m_new)
            l = alpha * l + jnp.sum(p, axis=1, keepdims=True)
            acc = alpha * acc + _dot(p.astype(BF16), vv)
            return m_new, l, acc

        m0 = jnp.full((rep * tq, 1), NEG, F32)
        l0 = jnp.zeros((rep * tq, 1), F32)
        a0 = jnp.zeros((rep * tq, A_DH), F32)
        _, l, acc = lax.fori_loop(0, nk, attend, (m0, l0, a0))
        o = acc / l
        for r in range(rep):
            o_ref[:, (g * rep + r) * A_DH:(g * rep + r + 1) * A_DH] = (
                o[r * tq:(r + 1) * tq].astype(o_ref.dtype))


def _dsa_prompt(q, k, v, qi, kia, kib, small, *, batch, seq, tq=128, ts=512):
    topk = min(TOPK, seq // 4)
    nq = seq // tq

    def rows(width):
        return pl.BlockSpec((tq, width), lambda b, i: (b * nq + i, 0))

    def whole(width):
        return pl.BlockSpec((seq, width), lambda b, i: (b, 0))

    return pl.pallas_call(
        functools.partial(_dsa_prompt_kernel, tq=tq, ts=ts, topk=topk),
        out_shape=jax.ShapeDtypeStruct((batch * seq, A_HEADS * A_DH), BF16),
        grid=(batch, nq),
        in_specs=[rows(A_HEADS * A_DH), whole(A_KV * A_DH), whole(A_KV * A_DH),
                  rows(IDX_HEADS * IDX_DIM), whole(LANES), whole(LANES), rows(LANES)],
        out_specs=rows(A_HEADS * A_DH),
        scratch_shapes=[pltpu.VMEM((seq // ts, tq, ts), F32)],
        compiler_params=_cparams(("arbitrary", "arbitrary")),
        name="dsa_prompt",
    )(q, k, v, qi, kia, kib, small)


def _dsa_select_kernel(pt_ref, qi_ref, wi_ref, kinew_ref, pool_ref, idx_ref,
                       kbuf_ref, row_ref, sc_ref, sem_ref, *, layer, n_pages, topk, chunk):
    b = pl.program_id(0)
    nb = pl.num_programs(0)
    past = n_pages * PAGE

    def page_copy(bb, p, slot):
        return pltpu.make_async_copy(pool_ref.at[layer, pt_ref[bb, p]],
                                     kbuf_ref.at[slot, pl.ds(p * PAGE, PAGE)],
                                     sem_ref.at[slot])

    def start_all(bb, slot):
        def body(p, c):
            page_copy(bb, p, slot).start()
            return c
        lax.fori_loop(0, n_pages, body, 0)

    def wait_all(bb, slot):
        def body(p, c):
            page_copy(bb, p, slot).wait()
            return c
        lax.fori_loop(0, n_pages, body, 0)

    slot = b % 2

    @pl.when(b == 0)
    def _():
        start_all(b, slot)

    @pl.when(b + 1 < nb)
    def _():
        start_all(b + 1, 1 - slot)

    wait_all(b, slot)

    qi = qi_ref[0]
    w = wi_ref[0]
    for c in range(past // chunk):
        kk = kbuf_ref[slot, c * chunk:(c + 1) * chunk, :].astype(BF16)
        lg = jnp.maximum(_dot_nt(qi, kk), 0.0) * w
        row_ref[:, c * chunk:(c + 1) * chunk] = (
            jnp.sum(lg, axis=0, keepdims=True) * (IDX_HEADS ** -0.5))
    for p in range(n_pages):
        sc_ref[p:p + 1, :] = row_ref[:, p * PAGE:(p + 1) * PAGE]
    sc = sc_ref[...]

    kin = kinew_ref[0].astype(BF16).astype(F32)
    lnew = jnp.sum(qi.astype(F32) * kin, axis=1, keepdims=True)
    snew = jnp.sum(jnp.maximum(lnew, 0.0) * w, axis=0, keepdims=True) * (IDX_HEADS ** -0.5)

    def total(x):
        return jnp.sum(jnp.sum(x, axis=1, keepdims=True), axis=0, keepdims=True)

    def count_ge(t):
        return total((sc >= t).astype(I32)) + (snew >= t).astype(I32)

    hi = jnp.maximum(jnp.max(jnp.max(sc, axis=1, keepdims=True), axis=0, keepdims=True), snew)
    lo = jnp.minimum(jnp.min(jnp.min(sc, axis=1, keepdims=True), axis=0, keepdims=True), snew)
    thr = _bisect_threshold(count_ge, lo, hi, topk)

    gt = sc > thr
    eq = sc == thr
    n_gt = total(gt.astype(I32)) + (snew > thr).astype(I32)
    need = (topk - n_gt).astype(F32)
    u_incl = (_iota((PAGE, PAGE), 0) <= _iota((PAGE, PAGE), 1)).astype(BF16)
    pg = (n_pages, n_pages)
    l_strict = (_iota(pg, 1) < _iota(pg, 0)).astype(BF16)
    u_pages = (_iota(pg, 0) <= _iota(pg, 1)).astype(BF16)
    eqf = eq.astype(BF16)
    eq_rank = _dot(eqf, u_incl) + jnp.sum(_dot(l_strict, eqf), axis=1, keepdims=True)
    sel = jnp.logical_or(gt, jnp.logical_and(eq, eq_rank <= need))
    self_ = sel.astype(BF16)
    inrow = _dot(self_, u_incl)
    ones8 = jnp.ones((SUBLANES, PAGE), BF16)
    crow = _dot_nt(ones8, self_)
    cend = _dot(crow.astype(BF16), u_pages)
    cstart = cend - crow
    slots = topk
    r = (_iota((slots, 1), 0) + 1).astype(F32)
    onehot = jnp.logical_and(cstart[0:1, :] < r, r <= cend[0:1, :])
    ohf = onehot.astype(F32)
    rowsel = _dot(onehot.astype(BF16), (sel.astype(F32) * inrow).astype(BF16))
    qr = r - jnp.sum(ohf * cstart[0:1, :], axis=1, keepdims=True)
    lane = _iota((slots, PAGE), 1)
    off = jnp.sum(jnp.where(rowsel == qr, lane, 0), axis=1, keepdims=True)
    page = jnp.sum(jnp.where(onehot, _iota((slots, n_pages), 1), 0), axis=1, keepdims=True)
    has = jnp.sum(ohf, axis=1, keepdims=True) > 0.5
    idx_ref[0] = jnp.where(has, page * PAGE + off, past)


def _dsa_select(page_table, qi, wi, ki_new, pool_ki, *, layer):
    bd, n_pages = page_table.shape
    past = n_pages * PAGE
    topk = min(TOPK, (past + 1) // 4)
    grid_spec = pltpu.PrefetchScalarGridSpec(
        num_scalar_prefetch=1,
        grid=(bd,),
        in_specs=[pl.BlockSpec((1, IDX_HEADS, IDX_DIM), lambda b, pt: (b, 0, 0)),
                  pl.BlockSpec((1, IDX_HEADS, 1), lambda b, pt: (b, 0, 0)),
                  pl.BlockSpec((1, 1, IDX_DIM), lambda b, pt: (b, 0, 0)),
                  pl.BlockSpec(memory_space=pl.ANY)],
        out_specs=pl.BlockSpec((1, topk, 1), lambda b, pt: (b, 0, 0)),
        scratch_shapes=[pltpu.VMEM((2, past, IDX_DIM), F32),
                        pltpu.VMEM((1, past), F32),
                        pltpu.VMEM((n_pages, PAGE), F32),
                        pltpu.SemaphoreType.DMA((2,))],
    )
    return pl.pallas_call(
        functools.partial(_dsa_select_kernel, layer=layer, n_pages=n_pages, topk=topk,
                          chunk=min(2048, past)),
        out_shape=jax.ShapeDtypeStruct((bd, topk, 1), I32),
        grid_spec=grid_spec,
        compiler_params=_cparams(("arbitrary",)),
        name="dsa_select",
    )(page_table, qi, wi, ki_new, pool_ki)


def _dsa_gather_attend_kernel(idx_s, pt_s, q_ref, idxv_ref, knew_ref, vnew_ref, kpool_ref, vpool_ref,
                              o_ref, kbuf_ref, vbuf_ref, sem_ref, *, layer, n_pages, topk):
    b = pl.program_id(0)
    nb = pl.num_programs(0)
    past = n_pages * PAGE

    def copies(bb, r, slot):
        ip = jnp.minimum(idx_s[bb, r], past - 1)
        phys = pt_s[bb, ip // PAGE]
        off = ip % PAGE
        return (pltpu.make_async_copy(kpool_ref.at[layer, phys, off], kbuf_ref.at[slot, r],
                                      sem_ref.at[0, slot]),
                pltpu.make_async_copy(vpool_ref.at[layer, phys, off], vbuf_ref.at[slot, r],
                                      sem_ref.at[1, slot]))

    def start_all(bb, slot):
        def body(r, c):
            ck, cv = copies(bb, r, slot)
            ck.start()
            cv.start()
            return c
        lax.fori_loop(0, topk, body, 0)

    def wait_all(bb, slot):
        def body(r, c):
            ck, cv = copies(bb, r, slot)
            ck.wait()
            cv.wait()
            return c
        lax.fori_loop(0, topk, body, 0)

    slot = b % 2

    @pl.when(b == 0)
    def _():
        start_all(b, slot)

    @pl.when(b + 1 < nb)
    def _():
        start_all(b + 1, 1 - slot)

    wait_all(b, slot)

    q = q_ref[0]
    in_past = idxv_ref[0] < past
    rep = A_HEADS // A_KV
    head_group = _iota((A_HEADS, A_DH), 0) // rep
    out = jnp.zeros((A_HEADS, A_DH), F32)
    for g in range(A_KV):
        kg = jnp.where(in_past, kbuf_ref[slot, :, g, :], knew_ref[0, g:g + 1, :]).astype(BF16)
        vg = jnp.where(in_past, vbuf_ref[slot, :, g, :], vnew_ref[0, g:g + 1, :]).astype(BF16)
        s = _dot_nt(q, kg) * (A_DH ** -0.5)
        m = jnp.max(s, axis=1, keepdims=True)
        p = jnp.exp(s - m)
        p = p / jnp.sum(p, axis=1, keepdims=True)
        o = _dot(p.astype(BF16), vg)
        out = jnp.where(head_group == g, o, out)
    o_ref[0] = out.astype(o_ref.dtype)


def _dsa_gather_attend(idx, page_table, q, k_new, v_new, pool_k, pool_v, *, layer):
    bd, n_pages = page_table.shape
    topk = idx.shape[1]
    grid_spec = pltpu.PrefetchScalarGridSpec(
        num_scalar_prefetch=2,
        grid=(bd,),
        in_specs=[pl.BlockSpec((1, A_HEADS, A_DH), lambda b, i, p: (b, 0, 0)),
                  pl.BlockSpec((1, topk, 1), lambda b, i, p: (b, 0, 0)),
                  pl.BlockSpec((1, A_KV, A_DH), lambda b, i, p: (b, 0, 0)),
                  pl.BlockSpec((1, A_KV, A_DH), lambda b, i, p: (b, 0, 0)),
                  pl.BlockSpec(memory_space=pl.ANY),
                  pl.BlockSpec(memory_space=pl.ANY)],
        out_specs=pl.BlockSpec((1, A_HEADS, A_DH), lambda b, i, p: (b, 0, 0)),
        scratch_shapes=[pltpu.VMEM((2, topk, A_KV, A_DH), F32),
                        pltpu.VMEM((2, topk, A_KV, A_DH), F32),
                        pltpu.SemaphoreType.DMA((2, 2))],
    )
    return pl.pallas_call(
        functools.partial(_dsa_gather_attend_kernel, layer=layer, n_pages=n_pages, topk=topk),
        out_shape=jax.ShapeDtypeStruct((bd, A_HEADS, A_DH), BF16),
        grid_spec=grid_spec,
        compiler_params=_cparams(("arbitrary",)),
        name="dsa_gather_attend",
    )(idx.reshape(bd, topk), page_table, q, idx, k_new, v_new, pool_k, pool_v)


def _head_rms_gate(o, gain, gate):
    ms = jnp.mean(o * o, axis=-1, keepdims=True)
    return o * lax.rsqrt(ms + EPS) * gain * _silu(gate)


def _gla_prompt_kernel(qkr_ref, v_ref, sm_ref, w2_ref, b2_ref, gn_ref, o_ref, sfin_ref, s_ref,
                       *, tt, chunk):
    i = pl.program_id(1)
    ni = pl.num_programs(1)

    @pl.when(i == 0)
    def _():
        s_ref[...] = jnp.zeros_like(s_ref)

    hdk = B_HEADS * B_DK
    r_i = _iota((chunk, chunk), 0)
    c_i = _iota((chunk, chunk), 1)
    causal = c_i <= r_i
    tril = causal.astype(BF16)
    x = _dot(sm_ref[...].astype(BF16), w2_ref[...]) + b2_ref[...]
    log_a = _log_sigmoid(x) * (1.0 / B_TAU)
    for c in range(tt // chunk):
        rows = slice(c * chunk, (c + 1) * chunk)
        for h in range(B_HEADS):
            dk = slice(h * B_DK, (h + 1) * B_DK)
            dv = slice(h * B_DV, (h + 1) * B_DV)
            q = qkr_ref[rows, dk] * (B_DK ** -0.5)
            k = qkr_ref[rows, hdk + h * B_DK:hdk + (h + 1) * B_DK]
            gate = qkr_ref[rows, 2 * hdk + h * B_DV:2 * hdk + (h + 1) * B_DV]
            v = v_ref[rows, dv]
            bc = _dot_exact_lhs(tril, log_a[rows, dk])
            bl = bc[chunk - 1:chunk, :]
            qd = (q * jnp.exp(bc)).astype(BF16)
            kd = (k * jnp.exp(-bc)).astype(BF16)
            att = jnp.where(causal, _dot_nt(qd, kd), 0.0)
            s_old = s_ref[h]
            o = _dot(qd, s_old.astype(BF16)) + _dot(att.astype(BF16), v)
            kdec = k * jnp.exp(bl - bc)
            decay = jnp.broadcast_to(jnp.exp(bl), (chunk, B_DK)).T
            decay = jnp.concatenate([decay] * (B_DV // B_DK), axis=1)
            s_ref[h] = decay * s_old + _dot(kdec.T.astype(BF16), v)
            o_ref[rows, dv] = _head_rms_gate(o, gn_ref[...], gate).astype(o_ref.dtype)

    @pl.when(i == ni - 1)
    def _():
        sfin_ref[0] = s_ref[...]


def _gla_prompt(qkr, v, small, w2pad, b2, gn, *, batch, seq, tt=256, chunk=128):
    nt = seq // tt
    hdk, hdv = B_HEADS * B_DK, B_HEADS * B_DV

    def rows(width):
        return pl.BlockSpec((tt, width), lambda b, i: (b * nt + i, 0))

    def const(shape):
        return pl.BlockSpec(shape, lambda b, i: (0,) * len(shape))

    return pl.pallas_call(
        functools.partial(_gla_prompt_kernel, tt=tt, chunk=chunk),
        out_shape=[jax.ShapeDtypeStruct((batch * seq, hdv), BF16),
                   jax.ShapeDtypeStruct((batch, B_HEADS, B_DK, B_DV), F32)],
        grid=(batch, nt),
        in_specs=[rows(2 * hdk + hdv), rows(hdv), rows(LANES),
                  const((LANES, hdk)), const((1, hdk)), const((1, B_DV))],
        out_specs=[rows(hdv),
                   pl.BlockSpec((1, B_HEADS, B_DK, B_DV), lambda b, i: (b, 0, 0, 0))],
        scratch_shapes=[pltpu.VMEM((B_HEADS, B_DK, B_DV), F32)],
        compiler_params=_cparams(("arbitrary", "arbitrary")),
        name="gla_prompt",
    )(qkr, v, small, w2pad, b2.reshape(1, hdk), gn.reshape(1, B_DV))


def _row_to_col(row):
    n = row.shape[1]
    eye = _iota((n, n), 0) == _iota((n, n), 1)
    return jnp.sum(jnp.where(eye, jnp.broadcast_to(row, (n, n)), 0.0), axis=1, keepdims=True)


def _gla_step_kernel(qkr_ref, v_ref, sm_ref, w2_ref, b2_ref, gn_ref, s_ref, o_ref, snew_ref):
    hdk = B_HEADS * B_DK
    smb = jnp.broadcast_to(sm_ref[0], (SUBLANES, LANES)).astype(BF16)
    x = _dot(smb, w2_ref[...])[0:1, :] + b2_ref[...]
    log_a = _log_sigmoid(x) * (1.0 / B_TAU)
    for h in range(B_HEADS):
        dk = slice(h * B_DK, (h + 1) * B_DK)
        dv = slice(h * B_DV, (h + 1) * B_DV)
        g_row = log_a[:, dk]
        q_row = qkr_ref[0, :, dk] * (B_DK ** -0.5)
        k_row = qkr_ref[0, :, hdk + h * B_DK:hdk + (h + 1) * B_DK]
        gate = qkr_ref[0, :, 2 * hdk + h * B_DV:2 * hdk + (h + 1) * B_DV]
        v_row = v_ref[0, :, dv].astype(F32)
        qd_row = q_row * jnp.exp(g_row)
        kd_row = k_row * jnp.exp(-g_row)
        s_old = s_ref[0, h]
        snew_ref[0, h] = _row_to_col(jnp.exp(g_row)) * s_old + _row_to_col(k_row) * v_row
        att = jnp.sum(qd_row * kd_row, axis=1, keepdims=True)
        o = jnp.sum(_row_to_col(qd_row) * s_old, axis=0, keepdims=True) + att * v_row
        o_ref[0, :, dv] = _head_rms_gate(o, gn_ref[...], gate).astype(o_ref.dtype)


def _gla_step(qkr, v, small, w2pad, b2, gn, state):
    bd = qkr.shape[0]
    hdk, hdv = B_HEADS * B_DK, B_HEADS * B_DV

    def row(width):
        return pl.BlockSpec((1, 1, width), lambda b: (b, 0, 0))

    def const(shape):
        return pl.BlockSpec(shape, lambda b: (0,) * len(shape))

    st = pl.BlockSpec((1, B_HEADS, B_DK, B_DV), lambda b: (b, 0, 0, 0))
    return pl.pallas_call(
        _gla_step_kernel,
        out_shape=[jax.ShapeDtypeStruct((bd, 1, hdv), BF16),
                   jax.ShapeDtypeStruct((bd, B_HEADS, B_DK, B_DV), F32)],
        grid=(bd,),
        in_specs=[row(2 * hdk + hdv), row(hdv), row(LANES),
                  const((LANES, hdk)), const((1, hdk)), const((1, B_DV)), st],
        out_specs=[row(hdv), st],
        compiler_params=_cparams(("arbitrary",)),
        name="gla_step",
    )(qkr.reshape(bd, 1, -1), v.reshape(bd, 1, -1), small.reshape(bd, 1, -1),
      w2pad, b2.reshape(1, hdk), gn.reshape(1, B_DV), state)


def _head_rms_sig(h, gain, gate):
    ms = jnp.mean(h * h, axis=-1, keepdims=True)
    return h * lax.rsqrt(ms + EPS) * gain * _sigmoid(gate)


def _mlstm_prompt_kernel(qko_ref, v_ref, sm_ref, bias_ref, hn_ref, o_ref, cfin_ref, nfin_ref, mfin_ref,
                         c_ref, n_ref, m_ref, *, tt, chunk):
    i = pl.program_id(1)
    ni = pl.num_programs(1)

    @pl.when(i == 0)
    def _():
        c_ref[...] = jnp.zeros_like(c_ref)
        n_ref[...] = jnp.zeros_like(n_ref)
        m_ref[...] = jnp.zeros_like(m_ref)

    hdk = C_HEADS * C_DK
    r_i = _iota((chunk, chunk), 0)
    c_i = _iota((chunk, chunk), 1)
    causal = c_i <= r_i
    tril = causal.astype(BF16)
    pre = sm_ref[...] + bias_ref[...]
    for c in range(tt // chunk):
        rows = slice(c * chunk, (c + 1) * chunk)
        for h in range(C_HEADS):
            dk = slice(h * C_DK, (h + 1) * C_DK)
            dv = slice(h * C_DV, (h + 1) * C_DV)
            icol = pre[rows, h:h + 1]
            fcol = _log_sigmoid(pre[rows, C_HEADS + h:C_HEADS + h + 1])
            bmat = _dot_exact_lhs(tril, jnp.broadcast_to(fcol, (chunk, chunk)))
            bcol = bmat[:, 0:1]
            b_t = bmat.T
            i_t = jnp.broadcast_to(icol, (chunk, chunk)).T
            m_prev = m_ref[h, 0:1, 0:1]
            log_d = jnp.where(causal, bmat - b_t + i_t, NEG)
            inter = bcol + m_prev
            m_t = jnp.maximum(inter, jnp.max(log_d, axis=1, keepdims=True))
            dmat = jnp.exp(log_d - m_t)
            g = jnp.exp(inter - m_t)
            q = qko_ref[rows, dk]
            ks = qko_ref[rows, hdk + h * C_DK:hdk + (h + 1) * C_DK] * (C_DK ** -0.5)
            gate = qko_ref[rows, 2 * hdk + h * C_DV:2 * hdk + (h + 1) * C_DV]
            v = v_ref[rows, dv]
            qb = q.astype(BF16)
            qk = _dot_nt(qb, ks.astype(BF16)) * dmat
            c_old = c_ref[h]
            n_old = n_ref[h, 0:1, :]
            num = g * _dot(qb, c_old.astype(BF16)) + _dot(qk.astype(BF16), v)
            den = g * jnp.sum(q * n_old, axis=1, keepdims=True) + jnp.sum(qk, axis=1, keepdims=True)
            hh = num / jnp.maximum(jnp.abs(den), jnp.exp(-m_t))
            o_ref[rows, dv] = _head_rms_sig(hh, hn_ref[...], gate).astype(o_ref.dtype)
            m_last = m_t[chunk - 1:chunk, :]
            g_last = g[chunk - 1:chunk, :]
            wcol = jnp.exp(bcol[chunk - 1:chunk, :] - bcol + icol - m_last)
            kw = ks * wcol
            c_ref[h] = g_last * c_old + _dot(kw.T.astype(BF16), v)
            n_ref[h] = jnp.broadcast_to(g_last * n_old + jnp.sum(kw, axis=0, keepdims=True),
                                        (SUBLANES, C_DK))
            m_ref[h] = jnp.broadcast_to(m_last, (SUBLANES, LANES))

    @pl.when(i == ni - 1)
    def _():
        cfin_ref[0] = c_ref[...]
        nfin_ref[0] = n_ref[...]
        mfin_ref[0] = m_ref[...]


def _mlstm_prompt(qko, v, small, bias, hn, *, batch, seq, tt=256, chunk=128):
    nt = seq // tt
    hdk, hdv = C_HEADS * C_DK, C_HEADS * C_DV

    def rows(width):
        return pl.BlockSpec((tt, width), lambda b, i: (b * nt + i, 0))

    def const(shape):
        return pl.BlockSpec(shape, lambda b, i: (0,) * len(shape))

    def per_batch(shape):
        return pl.BlockSpec((1,) + shape, lambda b, i: (b,) + (0,) * len(shape))

    return pl.pallas_call(
        functools.partial(_mlstm_prompt_kernel, tt=tt, chunk=chunk),
        out_shape=[jax.ShapeDtypeStruct((batch * seq, hdv), BF16),
                   jax.ShapeDtypeStruct((batch, C_HEADS, C_DK, C_DV), F32),
                   jax.ShapeDtypeStruct((batch, C_HEADS, SUBLANES, C_DK), F32),
                   jax.ShapeDtypeStruct((batch, C_HEADS, SUBLANES, LANES), F32)],
        grid=(batch, nt),
        in_specs=[rows(2 * hdk + hdv), rows(hdv), rows(LANES), const((1, LANES)), const((1, C_DV))],
        out_specs=[rows(hdv), per_batch((C_HEADS, C_DK, C_DV)),
                   per_batch((C_HEADS, SUBLANES, C_DK)), per_batch((C_HEADS, SUBLANES, LANES))],
        scratch_shapes=[pltpu.VMEM((C_HEADS, C_DK, C_DV), F32),
                        pltpu.VMEM((C_HEADS, SUBLANES, C_DK), F32),
                        pltpu.VMEM((C_HEADS, SUBLANES, LANES), F32)],
        compiler_params=_cparams(("arbitrary", "arbitrary")),
        name="mlstm_prompt",
    )(qko, v, small, bias, hn.reshape(1, C_DV))


def _mlstm_step_kernel(qko_ref, v_ref, sm_ref, bias_ref, hn_ref, c_ref, n_ref, m_ref,
                       o_ref, cnew_ref, nnew_ref, mnew_ref):
    hdk = C_HEADS * C_DK
    pre = sm_ref[0] + bias_ref[...]
    m_all = m_ref[0]
    m_out = jnp.zeros_like(m_all)
    for h in range(C_HEADS):
        dk = slice(h * C_DK, (h + 1) * C_DK)
        dv = slice(h * C_DV, (h + 1) * C_DV)
        i_pre = pre[:, h:h + 1]
        log_f = _log_sigmoid(pre[:, C_HEADS + h:C_HEADS + h + 1])
        m_prev = m_all[:, h:h + 1]
        inter = log_f + m_prev
        m_t = jnp.maximum(inter, i_pre)
        d = jnp.exp(i_pre - m_t)
        g = jnp.exp(inter - m_t)
        q_row = qko_ref[0, :, dk]
        ks_row = qko_ref[0, :, hdk + h * C_DK:hdk + (h + 1) * C_DK] * (C_DK ** -0.5)
        gate = qko_ref[0, :, 2 * hdk + h * C_DV:2 * hdk + (h + 1) * C_DV]
        v_row = v_ref[0, :, dv].astype(F32)
        c_old = c_ref[0, h]
        n_old = n_ref[0, h:h + 1, :]
        qk = jnp.sum(q_row * ks_row, axis=1, keepdims=True) * d
        num = g * jnp.sum(_row_to_col(q_row) * c_old, axis=0, keepdims=True) + qk * v_row
        den = g * jnp.sum(q_row * n_old, axis=1, keepdims=True) + qk
        hh = num / jnp.maximum(jnp.abs(den), jnp.exp(-m_t))
        o_ref[0, :, dv] = _head_rms_sig(hh, hn_ref[...], gate).astype(o_ref.dtype)
        cnew_ref[0, h] = g * c_old + _row_to_col(d * ks_row) * v_row
        nnew_ref[0, h:h + 1, :] = g * n_old + d * ks_row
        m_out = jnp.where(_iota(m_all.shape, 1) == h, m_t, m_out)
    mnew_ref[0] = m_out


def _mlstm_step(qko, v, small, bias, hn, c0, n0, m0):
    bd = qko.shape[0]
    hdk, hdv = C_HEADS * C_DK, C_HEADS * C_DV

    def row(width):
        return pl.BlockSpec((1, 1, width), lambda b: (b, 0, 0))

    def const(shape):
        return pl.BlockSpec(shape, lambda b: (0,) * len(shape))

    cs = pl.BlockSpec((1, C_HEADS, C_DK, C_DV), lambda b: (b, 0, 0, 0))
    ns = pl.BlockSpec((1, C_HEADS, C_DK), lambda b: (b, 0, 0))
    return pl.pallas_call(
        _mlstm_step_kernel,
        out_shape=[jax.ShapeDtypeStruct((bd, 1, hdv), BF16),
                   jax.ShapeDtypeStruct((bd, C_HEADS, C_DK, C_DV), F32),
                   jax.ShapeDtypeStruct((bd, C_HEADS, C_DK), F32),
                   jax.ShapeDtypeStruct((bd, 1, C_HEADS), F32)],
        grid=(bd,),
        in_specs=[row(2 * hdk + hdv), row(hdv), row(LANES), const((1, LANES)), const((1, C_DV)),
                  cs, ns, row(C_HEADS)],
        out_specs=[row(hdv), cs, ns, row(C_HEADS)],
        compiler_params=_cparams(("arbitrary",)),
        name="mlstm_step",
    )(qko.reshape(bd, 1, -1), v.reshape(bd, 1, -1), small.reshape(bd, 1, -1), bias,
      hn.reshape(1, C_DV), c0, n0, m0.reshape(bd, 1, C_HEADS))


def _swa_prompt_kernel(sink_ref, q_ref, kvp_ref, kvc_ref, o_ref, *, w):
    j = pl.program_id(1)
    band = jnp.concatenate([kvp_ref[...], kvc_ref[...]], axis=0)
    kk = band[:, :LANES]
    vv = band[:, LANES:]
    lane = _iota((2 * w, LANES), 1)
    qi = _iota((w, 2 * w), 0)
    kj = _iota((w, 2 * w), 1)
    allowed = jnp.logical_and(kj >= qi, kj <= qi + w)
    allowed = jnp.logical_and(allowed, jnp.logical_or(j > 0, kj >= w))
    rep = D_HEADS // D_KV
    pairs = rep // 2
    allowed = jnp.concatenate([allowed] * pairs, axis=0)
    zero = jnp.zeros_like(kk)
    for g in range(D_KV):
        own = (lane // D_DH) == g
        k_own = jnp.where(own, kk, zero)
        v_own = jnp.where(own, vv, zero)
        k_oth = pltpu.roll(k_own, D_DH, 1)
        v_oth = pltpu.roll(v_own, D_DH, 1)
        k_lo, k_hi = (k_own, k_oth) if g == 0 else (k_oth, k_own)
        v_lo, v_hi = (v_own, v_oth) if g == 0 else (v_oth, v_own)
        qs = jnp.concatenate([q_ref[:, (g * pairs + p) * LANES:(g * pairs + p + 1) * LANES]
                              for p in range(pairs)], axis=0)
        outs = []
        for half, (kh, vh) in enumerate(((k_lo, v_lo), (k_hi, v_hi))):
            kh = kh.astype(BF16)
            vh = vh.astype(BF16)
            s = _dot_nt(qs, kh) * (D_DH ** -0.5)
            s = jnp.where(allowed, s, NEG)
            sink = jnp.concatenate(
                [jnp.full((w, 1), sink_ref[2 * (g * pairs + p) + half], F32) for p in range(pairs)],
                axis=0)
            m = jnp.maximum(jnp.max(s, axis=1, keepdims=True), sink)
            e = jnp.where(allowed, jnp.exp(s - m), 0.0)
            p_ = e / (jnp.sum(e, axis=1, keepdims=True) + jnp.exp(sink - m))
            outs.append(_dot(p_.astype(BF16), vh))
        o = outs[0] + outs[1]
        for p in range(pairs):
            o_ref[:, (g * pairs + p) * LANES:(g * pairs + p + 1) * LANES] = (
                o[p * w:(p + 1) * w].astype(o_ref.dtype))


def _swa_prompt(q, kv, sinks, *, batch, seq):
    w = WINDOW
    nb = seq // w
    grid_spec = pltpu.PrefetchScalarGridSpec(
        num_scalar_prefetch=0,
        grid=(batch, nb),
        in_specs=[pl.BlockSpec(memory_space=pltpu.SMEM),
                  pl.BlockSpec((w, D_HEADS * D_DH), lambda b, j: (b * nb + j, 0)),
                  pl.BlockSpec((w, 2 * LANES), lambda b, j: (b * nb + jnp.maximum(j - 1, 0), 0)),
                  pl.BlockSpec((w, 2 * LANES), lambda b, j: (b * nb + j, 0))],
        out_specs=pl.BlockSpec((w, D_HEADS * D_DH), lambda b, j: (b * nb + j, 0)),
    )
    return pl.pallas_call(
        functools.partial(_swa_prompt_kernel, w=w),
        out_shape=jax.ShapeDtypeStruct((batch * seq, D_HEADS * D_DH), BF16),
        grid_spec=grid_spec,
        compiler_params=_cparams(("arbitrary", "arbitrary")),
        name="swa_prompt",
    )(sinks, q, kv, kv)


def _swa_step_kernel(q_ref, kvn_ref, bk_ref, bv_ref, sink_ref, o_ref, nk_ref, nv_ref):
    w = bk_ref.shape[1]
    rep = D_HEADS // D_KV
    q = q_ref[0]
    dup = (_iota((D_DH, LANES), 1) % D_DH == _iota((D_DH, LANES), 0)).astype(BF16)
    q2 = _dot(q, dup)
    own = (_iota((D_HEADS, LANES), 1) // D_DH) == (_iota((D_HEADS, LANES), 0) // rep)
    qm = jnp.where(own, q2, 0.0)
    bk = bk_ref[0]
    bv = bv_ref[0]
    k_new = kvn_ref[0, :, :LANES]
    v_new = kvn_ref[0, :, LANES:]
    s = _dot_nt(qm.astype(BF16), bk.astype(BF16)) * (D_DH ** -0.5)
    s_new = jnp.sum(qm * k_new.astype(BF16).astype(F32), axis=1, keepdims=True) * (D_DH ** -0.5)
    sink = sink_ref[...]
    m = jnp.maximum(jnp.maximum(jnp.max(s, axis=1, keepdims=True), s_new), sink)
    e = jnp.exp(s - m)
    e_new = jnp.exp(s_new - m)
    den = jnp.sum(e, axis=1, keepdims=True) + e_new + jnp.exp(sink - m)
    o = _dot((e / den).astype(BF16), bv.astype(BF16))
    o = o + (e_new / den).astype(BF16).astype(F32) * v_new.astype(BF16).astype(F32)
    o = jnp.where(own, o, 0.0)
    o = o + pltpu.roll(o, D_DH, 1)
    o_ref[0] = o[:, :D_DH].astype(o_ref.dtype)
    nk_ref[0, 0:w - 1, :] = bk_ref[0, 1:w, :]
    nk_ref[0, w - 1:w, :] = k_new
    nv_ref[0, 0:w - 1, :] = bv_ref[0, 1:w, :]
    nv_ref[0, w - 1:w, :] = v_new


def _swa_step(q, kv_new, buf_k, buf_v, sinks):
    bd = q.shape[0]
    w = buf_k.shape[1]
    buf = pl.BlockSpec((1, w, LANES), lambda b: (b, 0, 0))
    return pl.pallas_call(
        _swa_step_kernel,
        out_shape=[jax.ShapeDtypeStruct((bd, D_HEADS, D_DH), BF16),
                   jax.ShapeDtypeStruct((bd, w, LANES), F32),
                   jax.ShapeDtypeStruct((bd, w, LANES), F32)],
        grid=(bd,),
        in_specs=[pl.BlockSpec((1, D_HEADS, D_DH), lambda b: (b, 0, 0)),
                  pl.BlockSpec((1, 1, 2 * LANES), lambda b: (b, 0, 0)),
                  buf, buf,
                  pl.BlockSpec((D_HEADS, 1), lambda b: (0, 0))],
        out_specs=[pl.BlockSpec((1, D_HEADS, D_DH), lambda b: (b, 0, 0)), buf, buf],
        compiler_params=_cparams(("arbitrary",)),
        name="swa_step",
    )(q.reshape(bd, D_HEADS, D_DH), kv_new.reshape(bd, 1, 2 * LANES),
      buf_k.reshape(bd, w, LANES), buf_v.reshape(bd, w, LANES), sinks.reshape(D_HEADS, 1))


def _even_weights(w, a_qn, a_kn, a_kin, b_w2):
    o = [0]
    for s in (A_HEADS * A_DH, A_KV * A_DH, A_KV * A_DH, IDX_HEADS * IDX_DIM, IDX_DIM, IDX_HEADS,
              B_HEADS * B_DK, B_HEADS * B_DK, B_HEADS * B_DV, B_RANK, B_HEADS * B_DV):
        o.append(o[-1] + s)
    d = w.shape[0]
    col = lambda i: w[:, o[i]:o[i + 1]]
    pad = LANES - (IDX_DIM + IDX_HEADS + B_RANK)
    return dict(
        qa=col(0).astype(BF16),
        ka=col(1).astype(BF16),
        va=col(2).astype(BF16),
        qi_vb=jnp.concatenate([col(3) * (IDX_DIM ** -0.5), col(8)], axis=1).astype(BF16),
        small=jnp.concatenate([col(4), col(5), col(9), jnp.zeros((d, pad), w.dtype)], axis=1).astype(BF16),
        qkr=jnp.concatenate([col(6), col(7), col(10)], axis=1).astype(BF16),
        qa_gain=jnp.tile(a_qn, A_HEADS),
        ka_gain=jnp.tile(a_kn, A_KV),
        small_gain=jnp.concatenate([a_kin, jnp.zeros((LANES - IDX_DIM,), F32)]),
        w2pad=jnp.zeros((LANES, B_HEADS * B_DK), F32).at[
            IDX_DIM + IDX_HEADS:IDX_DIM + IDX_HEADS + B_RANK].set(b_w2).astype(BF16),
    )


def _odd_weights(w, c_bi, c_bf, d_qn, d_kn):
    o = [0]
    for s in (C_HEADS * C_DK, C_HEADS * C_DK, C_HEADS * C_DV, C_HEADS, C_HEADS, C_HEADS * C_DV,
              D_HEADS * D_DH, D_KV * D_DH, D_KV * D_DH):
        o.append(o[-1] + s)
    d = w.shape[0]
    col = lambda i: w[:, o[i]:o[i + 1]]
    return dict(
        qko=jnp.concatenate([col(0), col(1), col(5)], axis=1).astype(BF16),
        vc=col(2).astype(BF16),
        small=jnp.concatenate([col(3), col(4), jnp.zeros((d, LANES - 2 * C_HEADS), w.dtype)],
                              axis=1).astype(BF16),
        qd=col(6).astype(BF16),
        kvd=jnp.concatenate([col(7), col(8)], axis=1).astype(BF16),
        qd_gain=jnp.tile(d_qn, D_HEADS),
        kd_gain=jnp.concatenate([jnp.tile(d_kn, D_KV), jnp.ones((LANES,), F32)]),
        bias=jnp.concatenate([c_bi, c_bf, jnp.zeros((LANES - 2 * C_HEADS,), F32)]).reshape(1, LANES),
    )


def _even_in(h, ew):
    q, = _proj(h, ew["qa"], "rms128", (BF16,), ew["qa_gain"])
    k32, k16 = _proj(h, ew["ka"], "rms128", (F32, BF16), ew["ka_gain"])
    v32, v16 = _proj(h, ew["va"], "raw", (F32, BF16))
    qi_vb, = _proj(h, ew["qi_vb"], "raw", (BF16,))
    small, kia, kib = _proj(h, ew["small"], "small_even", (F32, BF16, BF16), ew["small_gain"])
    qkr, = _proj(h, ew["qkr"], "raw", (F32,))
    n_qi = IDX_HEADS * IDX_DIM
    return dict(q=q, k32=k32, k16=k16, v32=v32, v16=v16, qi=qi_vb[:, :n_qi], vb=qi_vb[:, n_qi:],
                small=small, kia=kia, kib=kib, qkr=qkr)


def _odd_in(h, ow):
    qko, = _proj(h, ow["qko"], "raw", (F32,))
    vc, = _proj(h, ow["vc"], "raw", (BF16,))
    small, = _proj(h, ow["small"], "raw", (F32,))
    qd, = _proj(h, ow["qd"], "rms64", (BF16,), ow["qd_gain"])
    kv32, = _proj(h, ow["kvd"], "kd_vd", (F32,), ow["kd_gain"])
    return dict(qko=qko, vc=vc, small=small, qd=qd, kv32=kv32)


def kernel(x_prompt, x_sample, cache_a_k, cache_a_v, cache_a_kidx, state_b_s, state_c_c, state_c_n,
           state_c_m, cache_d_k, cache_d_v, state_ffn_conv, page_table, norm_mix, norm_ffn,
           even_w_in, even_w_out, a_q_norm, a_k_norm, a_kidx_norm, b_gate_w2, b_gate_b, b_head_norm,
           odd_w_in, odd_w_out, c_i_bias, c_f_bias, c_head_norm, d_q_norm, d_k_norm, d_sinks,
           ffn_w_up, ffn_conv_w, ffn_conv_b, ffn_w_down):
    bp, seq, d_model = x_prompt.shape
    bd = x_sample.shape[0]
    depth = norm_mix.shape[0]
    d_ff = ffn_w_down.shape[1]
    mp = bp * seq
    xp = x_prompt.reshape(mp, d_model)
    xs = x_sample.reshape(bd, d_model)
    hp = _rms_cast(xp, norm_mix[0], tm=512)
    hs = _rms_cast(xs, norm_mix[0], tm=bd)

    p_out = {k: [] for k in ("ak", "av", "aki", "bs", "cc", "cn", "cm", "dk", "dv", "conv")}
    s_out = {k: [] for k in p_out}
    for layer in range(depth):
        if layer % 2 == 0:
            e = layer // 2
            ew = _even_weights(even_w_in[e], a_q_norm[e], a_k_norm[e], a_kidx_norm[e], b_gate_w2[e])
            w_out = even_w_out[e].astype(BF16)
            z = _even_in(hp, ew)
            a_o = _dsa_prompt(z["q"], z["k16"], z["v16"], z["qi"], z["kia"], z["kib"], z["small"],
                              batch=bp, seq=seq)
            g_o, s_fin = _gla_prompt(z["qkr"], z["vb"], z["small"], ew["w2pad"], b_gate_b[e],
                                     b_head_norm[e], batch=bp, seq=seq)
            p_out["ak"].append(z["k32"].reshape(bp, seq, A_KV, A_DH))
            p_out["av"].append(z["v32"].reshape(bp, seq, A_KV, A_DH))
            p_out["aki"].append(z["small"][:, :IDX_DIM].reshape(bp, seq, IDX_DIM))
            p_out["bs"].append(s_fin)
            mix_p = (a_o, g_o)
            z = _even_in(hs, ew)
            small = z["small"]
            idx = _dsa_select(page_table,
                              z["qi"].reshape(bd, IDX_HEADS, IDX_DIM),
                              small[:, IDX_DIM:IDX_DIM + IDX_HEADS].reshape(bd, IDX_HEADS, 1),
                              small[:, :IDX_DIM].reshape(bd, 1, IDX_DIM),
                              cache_a_kidx, layer=e)
            a_o = _dsa_gather_attend(idx, page_table, z["q"].reshape(bd, A_HEADS, A_DH),
                                     z["k32"].reshape(bd, A_KV, A_DH), z["v32"].reshape(bd, A_KV, A_DH),
                                     cache_a_k, cache_a_v, layer=e)
            g_o, s_new = _gla_step(z["qkr"], z["vb"], small, ew["w2pad"], b_gate_b[e],
                                   b_head_norm[e], state_b_s[e])
            s_out["ak"].append(z["k32"].reshape(bd, 1, A_KV, A_DH))
            s_out["av"].append(z["v32"].reshape(bd, 1, A_KV, A_DH))
            s_out["aki"].append(small[:, :IDX_DIM].reshape(bd, 1, IDX_DIM))
            s_out["bs"].append(s_new)
            mix_s = (a_o.reshape(bd, A_HEADS * A_DH), g_o.reshape(bd, B_HEADS * B_DV))
        else:
            o = layer // 2
            ow = _odd_weights(odd_w_in[o], c_i_bias[o], c_f_bias[o], d_q_norm[o], d_k_norm[o])
            w_out = odd_w_out[o].astype(BF16)
            z = _odd_in(hp, ow)
            c_h, cf, nf, mf = _mlstm_prompt(z["qko"], z["vc"], z["small"], ow["bias"], c_head_norm[o],
                                            batch=bp, seq=seq)
            d_o = _swa_prompt(z["qd"], z["kv32"], d_sinks[o], batch=bp, seq=seq)
            kv = z["kv32"].reshape(bp, seq, 2, D_KV, D_DH)[:, seq - WINDOW:]
            p_out["cc"].append(cf)
            p_out["cn"].append(nf[:, :, 0, :])
            p_out["cm"].append(mf[:, :, 0, 0])
            p_out["dk"].append(kv[:, :, 0])
            p_out["dv"].append(kv[:, :, 1])
            mix_p = (c_h, d_o)
            z = _odd_in(hs, ow)
            c_h, c_new, n_new, m_new = _mlstm_step(z["qko"], z["vc"], z["small"], ow["bias"],
                                                   c_head_norm[o], state_c_c[o], state_c_n[o],
                                                   state_c_m[o])
            d_o, nbk, nbv = _swa_step(z["qd"], z["kv32"], cache_d_k[o], cache_d_v[o], d_sinks[o])
            s_out["cc"].append(c_new)
            s_out["cn"].append(n_new)
            s_out["cm"].append(m_new.reshape(bd, C_HEADS))
            s_out["dk"].append(nbk.reshape(bd, WINDOW, D_KV, D_DH))
            s_out["dv"].append(nbv.reshape(bd, WINDOW, D_KV, D_DH))
            mix_s = (c_h.reshape(bd, C_HEADS * C_DV), d_o.reshape(bd, D_HEADS * D_DH))

        xp, hfp = _outproj(mix_p[0], mix_p[1], w_out, xp, norm_ffn[layer])
        xs, hfs = _outproj(mix_s[0], mix_s[1], w_out, xs, norm_ffn[layer])

        wu = ffn_w_up[layer, :, :d_ff].astype(BF16)
        wg = ffn_w_up[layer, :, d_ff:].astype(BF16)
        wd = ffn_w_down[layer].astype(BF16)
        gain_next = norm_mix[layer + 1] if layer + 1 < depth else jnp.ones((d_model,), F32)
        xp, hp, glp = _ffn(xp, hfp, wu, wg, ffn_conv_w[layer], ffn_conv_b[layer], wd, gain_next,
                           jnp.zeros((bp, SUBLANES, d_ff), F32), seq_len=seq, stepwise=False)
        xs, hs, gls = _ffn(xs, hfs, wu, wg, ffn_conv_w[layer], ffn_conv_b[layer], wd, gain_next,
                           state_ffn_conv[layer].swapaxes(0, 1), seq_len=1, stepwise=True)
        tiles = glp.shape[0] // bp
        p_out["conv"].append(glp.reshape(bp, tiles, SUBLANES, d_ff)[:, -1, SUBLANES - (CONV_W - 1):])
        s_out["conv"].append(jnp.stack([state_ffn_conv[layer][:, 1], gls], axis=1))

    order = ("ak", "av", "aki", "bs", "cc", "cn", "cm", "dk", "dv", "conv")
    return (xp.reshape(bp, seq, d_model), xs.reshape(bd, 1, d_model),
            *[jnp.stack(p_out[k]) for k in order],
            *[jnp.stack(s_out[k]) for k in order])
```

```python
import functools

import jax
import jax.numpy as jnp
from jax import lax
from jax.experimental import pallas as pl
from jax.experimental.pallas import tpu as pltpu

F32 = jnp.float32
BF16 = jnp.bfloat16
I32 = jnp.int32

EPS = 1e-6
NEG = -1e30

PAGE = 128
A_HEADS, A_KV, A_DH = 8, 4, 128
IDX_HEADS, IDX_DIM = 16, 64
TOPK = 256
B_HEADS, B_DK, B_DV, B_RANK, B_TAU = 4, 128, 256, 16, 16.0
C_HEADS, C_DK, C_DV = 4, 128, 256
D_HEADS, D_KV, D_DH = 16, 2, 64
WINDOW = 128
CONV_W = 3

VMEM_LIMIT = 56 * 1024 * 1024
LANES = 128
SUBLANES = 8


def _cparams(sem):
    return pltpu.CompilerParams(dimension_semantics=sem, vmem_limit_bytes=VMEM_LIMIT)


def _dot(a, b):
    return jnp.dot(a, b, preferred_element_type=F32)


def _dot_nt(a, b):
    return lax.dot_general(a, b, (((1,), (1,)), ((), ())), preferred_element_type=F32)


def _split3(x):
    hi = x.astype(BF16)
    r1 = x - hi.astype(F32)
    mid = r1.astype(BF16)
    lo = (r1 - mid.astype(F32)).astype(BF16)
    return hi, mid, lo


def _dot_exact_lhs(a_bf16, x):
    hi, mid, lo = _split3(x)
    return _dot(a_bf16, hi) + _dot(a_bf16, mid) + _dot(a_bf16, lo)


def _log_sigmoid(x):
    return jnp.minimum(x, 0.0) - jnp.log(1.0 + jnp.exp(-jnp.abs(x)))


def _sigmoid(x):
    return 1.0 / (1.0 + jnp.exp(-x))


def _silu(x):
    return x * _sigmoid(x)


def _iota(shape, dim):
    return lax.broadcasted_iota(I32, shape, dim)


def _rms_cast_kernel(x_ref, g_ref, o_ref):
    x = x_ref[...]
    ms = jnp.mean(x * x, axis=-1, keepdims=True)
    o_ref[...] = (x * lax.rsqrt(ms + EPS) * g_ref[...]).astype(o_ref.dtype)


def _rms_cast(x, gain, tm):
    m, d = x.shape
    return pl.pallas_call(
        _rms_cast_kernel,
        out_shape=jax.ShapeDtypeStruct((m, d), BF16),
        grid=(m // tm,),
        in_specs=[pl.BlockSpec((tm, d), lambda i: (i, 0)),
                  pl.BlockSpec((1, d), lambda i: (0, 0))],
        out_specs=pl.BlockSpec((tm, d), lambda i: (i, 0)),
        compiler_params=_cparams(("arbitrary",)),
        name="rms_cast",
    )(x, gain.reshape(1, d))


def _group_rms(acc, gain, width):
    tn = acc.shape[1]
    outs = []
    for c in range(tn // LANES):
        blk = acc[:, c * LANES:(c + 1) * LANES]
        sq = blk * blk
        if width == LANES:
            ms = jnp.sum(sq, axis=-1, keepdims=True) * (1.0 / width)
        else:
            low = _iota(blk.shape, 1) < width
            s_lo = jnp.sum(jnp.where(low, sq, 0.0), axis=-1, keepdims=True)
            s_hi = jnp.sum(jnp.where(low, 0.0, sq), axis=-1, keepdims=True)
            ms = jnp.where(low, s_lo, s_hi) * (1.0 / width)
        outs.append(blk * lax.rsqrt(ms + EPS) * gain[:, c * LANES:(c + 1) * LANES])
    return outs[0] if len(outs) == 1 else jnp.concatenate(outs, axis=1)


def _proj_kernel(h_ref, w_ref, *rest, mode, n_out):
    acc = _dot(h_ref[...], w_ref[...])
    outs = rest[-n_out:]
    if mode == "raw":
        y = acc
    elif mode == "rms128":
        y = _group_rms(acc, rest[0][...], 128)
    elif mode == "rms64":
        y = _group_rms(acc, rest[0][...], 64)
    elif mode == "kd_vd":
        y = jnp.concatenate([_group_rms(acc[:, :LANES], rest[0][:, :LANES], 64), acc[:, LANES:]], axis=1)
    elif mode == "small_even":
        low = _iota(acc.shape, 1) < IDX_DIM
        ms = jnp.sum(jnp.where(low, acc * acc, 0.0), axis=-1, keepdims=True) * (1.0 / IDX_DIM)
        kin = acc * lax.rsqrt(ms + EPS) * rest[0][...]
        y = jnp.where(low, kin, acc)
        ka = jnp.where(low, kin, 0.0)
        outs[1][...] = ka.astype(BF16)
        outs[2][...] = pltpu.roll(ka, IDX_DIM, 1).astype(BF16)
        outs[0][...] = y
        return
    else:
        raise ValueError(mode)
    for o in outs:
        o[...] = y.astype(o.dtype)


def _proj(h, w, mode, out_dtypes, gain=None, tm=1024, tn=512):
    m, k = h.shape
    n = w.shape[1]
    tm = min(tm, m)
    tn = min(tn, n)
    in_specs = [pl.BlockSpec((tm, k), lambda i, j: (i, 0)),
                pl.BlockSpec((k, tn), lambda i, j: (0, j))]
    args = [h, w]
    if gain is not None:
        in_specs.append(pl.BlockSpec((1, tn), lambda i, j: (0, j)))
        args.append(gain.reshape(1, n).astype(F32))
    outs = pl.pallas_call(
        functools.partial(_proj_kernel, mode=mode, n_out=len(out_dtypes)),
        out_shape=[jax.ShapeDtypeStruct((m, n), dt) for dt in out_dtypes],
        grid=(m // tm, n // tn),
        in_specs=in_specs,
        out_specs=[pl.BlockSpec((tm, tn), lambda i, j: (i, j)) for _ in out_dtypes],
        compiler_params=_cparams(("arbitrary", "arbitrary")),
        name="proj_" + mode,
    )(*args)
    return outs


def _outproj_kernel(a_ref, b_ref, wa_ref, wb_ref, x_ref, g_ref, xo_ref, ho_ref):
    y = x_ref[...] + _dot(a_ref[...], wa_ref[...]) + _dot(b_ref[...], wb_ref[...])
    xo_ref[...] = y
    ms = jnp.mean(y * y, axis=-1, keepdims=True)
    ho_ref[...] = (y * lax.rsqrt(ms + EPS) * g_ref[...]).astype(ho_ref.dtype)


def _outproj(a, b, w, x, gain, tm=512):
    m, d = x.shape
    ka, kb = a.shape[1], b.shape[1]
    tm = min(tm, m)
    return pl.pallas_call(
        _outproj_kernel,
        out_shape=[jax.ShapeDtypeStruct((m, d), F32), jax.ShapeDtypeStruct((m, d), BF16)],
        grid=(m // tm,),
        in_specs=[pl.BlockSpec((tm, ka), lambda i: (i, 0)),
                  pl.BlockSpec((tm, kb), lambda i: (i, 0)),
                  pl.BlockSpec((ka, d), lambda i: (0, 0)),
                  pl.BlockSpec((kb, d), lambda i: (0, 0)),
                  pl.BlockSpec((tm, d), lambda i: (i, 0)),
                  pl.BlockSpec((1, d), lambda i: (0, 0))],
        out_specs=[pl.BlockSpec((tm, d), lambda i: (i, 0)),
                   pl.BlockSpec((tm, d), lambda i: (i, 0))],
        compiler_params=_cparams(("arbitrary",)),
        name="outproj",
    )(a, b, w[:ka], w[ka:], x, gain.reshape(1, d))


def _ffn_up_kernel(h_ref, wu_ref, wg_ref, cw_ref, cb_ref, st_ref, a_ref, gl_ref, gbuf_ref, carry_ref,
                   *, tiles_per_seq, stepwise):
    i = pl.program_id(0)
    f = pl.program_id(1)
    tm = h_ref.shape[0]
    h = h_ref[...]
    u = _dot(h, wu_ref[...])
    g = _dot(h, wg_ref[...])
    cw = cw_ref[...]
    if stepwise:
        gm2 = st_ref[0]
        gm1 = st_ref[1]
        gl_ref[...] = g
    else:
        first = (i % tiles_per_seq) == 0

        @pl.when(first)
        def _():
            gbuf_ref[0:SUBLANES, :] = st_ref[0]

        @pl.when(jnp.logical_not(first))
        def _():
            gbuf_ref[0:SUBLANES, :] = carry_ref[f]

        gbuf_ref[SUBLANES:SUBLANES + tm, :] = g
        gm1 = gbuf_ref[SUBLANES - 1:SUBLANES - 1 + tm, :]
        gm2 = gbuf_ref[SUBLANES - 2:SUBLANES - 2 + tm, :]
        tail = g[tm - SUBLANES:, :]
        carry_ref[f] = tail
        gl_ref[0] = tail
    gc = cb_ref[...] + cw[0:1, :] * gm2 + cw[1:2, :] * gm1 + cw[2:3, :] * g
    a_ref[...] = (_silu(gc) * u).astype(a_ref.dtype)


def _ffn_down_kernel(a_ref, wd_ref, x_ref, xo_ref):
    xo_ref[...] = x_ref[...] + _dot(a_ref[...], wd_ref[...])


def _ffn(x, h, wu, wg, cw, cb, wd, state, *, seq_len, stepwise, tm=1024, tf=512, tn=512):
    m, d = x.shape
    ff = wu.shape[1]
    tm = min(tm, m)
    n_i, n_f = m // tm, ff // tf
    tiles_per_seq = max(seq_len // tm, 1)
    if stepwise:
        st_spec = pl.BlockSpec((2, tm, tf), lambda i, f: (0, i, f))
        gl_shape = jax.ShapeDtypeStruct((m, ff), F32)
        gl_spec = pl.BlockSpec((tm, tf), lambda i, f: (i, f))
    else:
        st_spec = pl.BlockSpec((1, SUBLANES, tf), lambda i, f: (i // tiles_per_seq, 0, f))
        gl_shape = jax.ShapeDtypeStruct((n_i, SUBLANES, ff), F32)
        gl_spec = pl.BlockSpec((1, SUBLANES, tf), lambda i, f: (i, 0, f))
    a, gl = pl.pallas_call(
        functools.partial(_ffn_up_kernel, tiles_per_seq=tiles_per_seq, stepwise=stepwise),
        out_shape=[jax.ShapeDtypeStruct((m, ff), BF16), gl_shape],
        grid=(n_i, n_f),
        in_specs=[pl.BlockSpec((tm, d), lambda i, f: (i, 0)),
                  pl.BlockSpec((d, tf), lambda i, f: (0, f)),
                  pl.BlockSpec((d, tf), lambda i, f: (0, f)),
                  pl.BlockSpec((CONV_W, tf), lambda i, f: (0, f)),
                  pl.BlockSpec((1, tf), lambda i, f: (0, f)),
                  st_spec],
        out_specs=[pl.BlockSpec((tm, tf), lambda i, f: (i, f)), gl_spec],
        scratch_shapes=[pltpu.VMEM((SUBLANES + tm, tf), F32),
                        pltpu.VMEM((n_f, SUBLANES, tf), F32)],
        compiler_params=_cparams(("arbitrary", "arbitrary")),
        name="ffn_up_step" if stepwise else "ffn_up",
    )(h, wu, wg, cw, cb.reshape(1, ff), state)
    xo = pl.pallas_call(
        _ffn_down_kernel,
        out_shape=jax.ShapeDtypeStruct((m, d), F32),
        grid=(n_i, d // tn),
        in_specs=[pl.BlockSpec((tm, ff), lambda i, n: (i, 0)),
                  pl.BlockSpec((ff, tn), lambda i, n: (0, n)),
                  pl.BlockSpec((tm, tn), lambda i, n: (i, n))],
        out_specs=pl.BlockSpec((tm, tn), lambda i, n: (i, n)),
        compiler_params=_cparams(("arbitrary", "arbitrary")),
        name="ffn_down",
    )(a, wd, x)
    return xo, gl


def _bisect_threshold(count_ge, lo, hi, k, max_iter=80):
    def cond(c):
        it, _, _, done = c
        return jnp.logical_and(it < max_iter, jnp.min(done) == 0)

    def body(c):
        it, lo, hi, done = c
        mid = 0.5 * (lo + hi)
        cnt = count_ge(mid)
        ge = cnt >= k
        stuck = jnp.logical_or(mid <= lo, mid >= hi)
        new_done = jnp.logical_or(jnp.logical_and(ge, cnt == k), stuck)
        lo = jnp.where(ge, mid, lo)
        hi = jnp.where(ge, hi, mid)
        return it + 1, lo, hi, jnp.maximum(done, new_done.astype(I32))

    done0 = (count_ge(lo) == k).astype(I32)
    _, lo, _, _ = lax.while_loop(cond, body, (jnp.int32(0), lo, hi, done0))
    return lo


def _narrow_bracket(count_ge, lo, hi, k, ways=8, passes=6):
    def body(_, c):
        lo, hi = c
        width = hi - lo
        new_lo, new_hi = lo, hi
        for j in range(1, ways):
            t = lo + width * (j / ways)
            ge = count_ge(t) >= k
            new_lo = jnp.where(ge, jnp.maximum(new_lo, t), new_lo)
            new_hi = jnp.where(ge, new_hi, jnp.minimum(new_hi, t))
        return new_lo, new_hi

    return lax.fori_loop(0, passes, body, (lo, hi))


def _dsa_prompt_kernel(q_ref, k_ref, v_ref, qi_ref, kia_ref, kib_ref, sm_ref, o_ref, sc_ref,
                       *, tq, ts, topk):
    i = pl.program_id(1)
    q0 = i * tq
    nk = (q0 + tq + ts - 1) // ts
    qpos = q0 + _iota((tq, 1), 0)
    sm = sm_ref[...]
    wcols = [sm[:, IDX_DIM + h:IDX_DIM + h + 1] for h in range(IDX_HEADS)]

    def score_tile(kt, carry):
        ks = pl.multiple_of(kt * ts, ts)
        ka = kia_ref[pl.ds(ks, ts), :]
        kb = kib_ref[pl.ds(ks, ts), :]
        acc = jnp.zeros((tq, ts), F32)
        for p in range(IDX_HEADS // 2):
            qp = qi_ref[:, p * LANES:(p + 1) * LANES]
            acc = acc + wcols[2 * p] * jnp.maximum(_dot_nt(qp, ka), 0.0)
            acc = acc + wcols[2 * p + 1] * jnp.maximum(_dot_nt(qp, kb), 0.0)
        kpos = ks + _iota((1, ts), 1)
        sc_ref[kt] = jnp.where(kpos <= qpos, acc * (IDX_HEADS ** -0.5), NEG)
        return carry

    lax.fori_loop(0, nk, score_tile, 0)

    def reduce_tiles(fn, comb, init):
        def body(kt, c):
            x = fn(sc_ref[kt])
            for j in range(ts // LANES):
                c = comb(c, x[:, j * LANES:(j + 1) * LANES])
            return c
        return lax.fori_loop(0, nk, body, jnp.full((tq, LANES), init, F32))

    def count_ge(t):
        part = reduce_tiles(lambda s: jnp.where(s >= t, 1.0, 0.0), lambda a, b: a + b, 0.0)
        return jnp.sum(part, axis=1, keepdims=True)

    hi = jnp.max(reduce_tiles(lambda s: s, jnp.maximum, NEG), axis=1, keepdims=True)
    lo = jnp.min(reduce_tiles(lambda s: jnp.where(s > 0.5 * NEG, s, -NEG), jnp.minimum, -NEG),
                 axis=1, keepdims=True)
    thr = _bisect_threshold(count_ge, lo, hi, float(topk))
    thr = jnp.where(qpos + 1 <= topk, NEG, thr)

    def to_bias(kt, carry):
        s = sc_ref[kt]
        keep = jnp.logical_and(s >= thr, s > 0.5 * NEG)
        sc_ref[kt] = jnp.where(keep, 0.0, NEG)
        return carry

    lax.fori_loop(0, nk, to_bias, 0)

    scale2 = (A_DH ** -0.5) * 1.4426950408889634
    rep = A_HEADS // A_KV
    for g in range(A_KV):
        qg = jnp.concatenate(
            [q_ref[:, (g * rep + r) * A_DH:(g * rep + r + 1) * A_DH] for r in range(rep)], axis=0)

        def attend(kt, c, g=g, qg=qg):
            m, l, acc = c
            ks = pl.multiple_of(kt * ts, ts)
            kk = k_ref[pl.ds(ks, ts), g * A_DH:(g + 1) * A_DH]
            vv = v_ref[pl.ds(ks, ts), g * A_DH:(g + 1) * A_DH]
            bias = sc_ref[kt]
            s = _dot_nt(qg, kk) * scale2 + jnp.concatenate([bias] * rep, axis=0)
            m_new = jnp.maximum(m, jnp.max(s, axis=1, keepdims=True))
            p = jnp.exp2(s - m_new)
            alpha = jnp.exp2(m - m_new)
            l = alpha * l + jnp.sum(p, axis=1, keepdims=True)
            acc = alpha * acc + _dot(p.astype(BF16), vv)
            return m_new, l, acc

        m0 = jnp.full((rep * tq, 1), NEG, F32)
        l0 = jnp.zeros((rep * tq, 1), F32)
        a0 = jnp.zeros((rep * tq, A_DH), F32)
        _, l, acc = lax.fori_loop(0, nk, attend, (m0, l0, a0))
        o = acc / l
        for r in range(rep):
            o_ref[:, (g * rep + r) * A_DH:(g * rep + r + 1) * A_DH] = (
                o[r * tq:(r + 1) * tq].astype(o_ref.dtype))


def _dsa_prompt(q, k, v, qi, kia, kib, small, *, batch, seq, tq=128, ts=512):
    topk = min(TOPK, seq // 4)
    nq = seq // tq

    def rows(width):
        return pl.BlockSpec((tq, width), lambda b, i: (b * nq + i, 0))

    def whole(width):
        return pl.BlockSpec((seq, width), lambda b, i: (b, 0))

    return pl.pallas_call(
        functools.partial(_dsa_prompt_kernel, tq=tq, ts=ts, topk=topk),
        out_shape=jax.ShapeDtypeStruct((batch * seq, A_HEADS * A_DH), BF16),
        grid=(batch, nq),
        in_specs=[rows(A_HEADS * A_DH), whole(A_KV * A_DH), whole(A_KV * A_DH),
                  rows(IDX_HEADS * IDX_DIM), whole(LANES), whole(LANES), rows(LANES)],
        out_specs=rows(A_HEADS * A_DH),
        scratch_shapes=[pltpu.VMEM((seq // ts, tq, ts), F32)],
        compiler_params=_cparams(("arbitrary", "arbitrary")),
        name="dsa_prompt",
    )(q, k, v, qi, kia, kib, small)


def _dsa_select_kernel(pt_ref, ptv_ref, qi_ref, wi_ref, kinew_ref, pool_ref, idx_ref,
                       kbuf_ref, sc_ref, sem_ref, *, layer, n_pages, topk):
    b = pl.program_id(0)
    nb = pl.num_programs(0)
    past = n_pages * PAGE

    def page_copy(bb, p, slot):
        return pltpu.make_async_copy(pool_ref.at[layer, pt_ref[bb, p]], kbuf_ref.at[slot, p],
                                     sem_ref.at[slot])

    def start_all(bb, slot):
        def body(p, c):
            page_copy(bb, p, slot).start()
            return c
        lax.fori_loop(0, n_pages, body, 0)

    def wait_all(bb, slot):
        def body(p, c):
            page_copy(bb, p, slot).wait()
            return c
        lax.fori_loop(0, n_pages, body, 0)

    slot = b % 2

    @pl.when(b == 0)
    def _():
        start_all(b, slot)

    @pl.when(b + 1 < nb)
    def _():
        start_all(b + 1, 1 - slot)

    wait_all(b, slot)

    qi = qi_ref[0]
    w = wi_ref[0]
    def score_page(p, carry):
        kt = kbuf_ref[slot, p].astype(BF16)
        lg = jnp.maximum(_dot(qi, kt), 0.0) * w
        sc_ref[pl.ds(p, 1), :] = jnp.sum(lg, axis=0, keepdims=True) * (IDX_HEADS ** -0.5)
        return carry

    lax.fori_loop(0, n_pages, score_page, 0, unroll=4)
    sc = sc_ref[...]

    kin = kinew_ref[0].astype(BF16).astype(F32)
    lnew = jnp.sum(qi.astype(F32) * kin, axis=1, keepdims=True)
    snew = jnp.sum(jnp.maximum(lnew, 0.0) * w, axis=0, keepdims=True) * (IDX_HEADS ** -0.5)

    def total(x):
        return jnp.sum(jnp.sum(x, axis=1, keepdims=True), axis=0, keepdims=True)

    def count_ge(t):
        return total(jnp.where(sc >= t, 1.0, 0.0)) + jnp.where(snew >= t, 1.0, 0.0)

    hi = jnp.maximum(jnp.max(jnp.max(sc, axis=1, keepdims=True), axis=0, keepdims=True), snew)
    lo = jnp.minimum(jnp.min(jnp.min(sc, axis=1, keepdims=True), axis=0, keepdims=True), snew)
    lo, hi = _narrow_bracket(count_ge, lo, hi, float(topk))
    thr = _bisect_threshold(count_ge, lo, hi, float(topk))

    gt = sc > thr
    eq = sc == thr
    n_gt = total(jnp.where(gt, 1.0, 0.0)) + jnp.where(snew > thr, 1.0, 0.0)
    need = topk - n_gt
    u_incl = (_iota((PAGE, PAGE), 0) <= _iota((PAGE, PAGE), 1)).astype(BF16)
    pg = (n_pages, n_pages)
    l_strict = (_iota(pg, 1) < _iota(pg, 0)).astype(BF16)
    u_pages = (_iota(pg, 0) <= _iota(pg, 1)).astype(BF16)
    eqf = eq.astype(BF16)
    eq_rank = _dot(eqf, u_incl) + jnp.sum(_dot(l_strict, eqf), axis=1, keepdims=True)
    sel = jnp.logical_or(gt, jnp.logical_and(eq, eq_rank <= need))
    self_ = sel.astype(BF16)
    inrow = _dot(self_, u_incl)
    ones8 = jnp.ones((SUBLANES, PAGE), BF16)
    crow = _dot_nt(ones8, self_)
    cend = _dot(crow.astype(BF16), u_pages)
    cstart = cend - crow
    slots = topk
    r = (_iota((slots, 1), 0) + 1).astype(F32)
    onehot = jnp.logical_and(cstart[0:1, :] < r, r <= cend[0:1, :])
    ohf = onehot.astype(F32)
    rowsel = _dot(onehot.astype(BF16), (sel.astype(F32) * inrow).astype(BF16))
    qr = r - jnp.sum(ohf * cstart[0:1, :], axis=1, keepdims=True)
    lane = _iota((slots, PAGE), 1)
    off = jnp.sum(jnp.where(rowsel == qr, lane, 0), axis=1, keepdims=True)
    page = jnp.sum(jnp.where(onehot, _iota((slots, n_pages), 1), 0), axis=1, keepdims=True)
    has = jnp.sum(ohf, axis=1, keepdims=True) > 0.5
    phys = jnp.sum(ohf * ptv_ref[0].astype(F32), axis=1, keepdims=True).astype(I32)
    idx_ref[0] = jnp.where(has, phys * PAGE + off, -1)


def _dsa_select(page_table, qi, wi, ki_new, pool_ki_t, *, layer):
    bd, n_pages = page_table.shape
    past = n_pages * PAGE
    topk = min(TOPK, (past + 1) // 4)
    grid_spec = pltpu.PrefetchScalarGridSpec(
        num_scalar_prefetch=1,
        grid=(bd,),
        in_specs=[pl.BlockSpec((1, 1, n_pages), lambda b, pt: (b, 0, 0)),
                  pl.BlockSpec((1, IDX_HEADS, IDX_DIM), lambda b, pt: (b, 0, 0)),
                  pl.BlockSpec((1, IDX_HEADS, 1), lambda b, pt: (b, 0, 0)),
                  pl.BlockSpec((1, 1, IDX_DIM), lambda b, pt: (b, 0, 0)),
                  pl.BlockSpec(memory_space=pl.ANY)],
        out_specs=pl.BlockSpec((1, topk, 1), lambda b, pt: (b, 0, 0)),
        scratch_shapes=[pltpu.VMEM((2, n_pages, IDX_DIM, PAGE), F32),
                        pltpu.VMEM((n_pages, PAGE), F32),
                        pltpu.SemaphoreType.DMA((2,))],
    )
    return pl.pallas_call(
        functools.partial(_dsa_select_kernel, layer=layer, n_pages=n_pages, topk=topk),
        out_shape=jax.ShapeDtypeStruct((bd, topk, 1), I32),
        grid_spec=grid_spec,
        compiler_params=pltpu.CompilerParams(dimension_semantics=("arbitrary",),
                                             vmem_limit_bytes=VMEM_LIMIT, disable_bounds_checks=True),
        name="dsa_select",
    )(page_table, page_table.reshape(bd, 1, n_pages), qi, wi, ki_new, pool_ki_t)


def _dsa_gather_attend_kernel(idx_s, q_ref, idxv_ref, knew_ref, vnew_ref, kpool_ref, vpool_ref,
                              o_ref, kbuf_ref, vbuf_ref, sem_ref, *, layer, topk):
    b = pl.program_id(0)
    nb = pl.num_programs(0)

    def copies(bb, r, slot):
        row = jnp.maximum(idx_s[bb, r], 0)
        return (pltpu.make_async_copy(kpool_ref.at[layer, row], kbuf_ref.at[slot, r],
                                      sem_ref.at[0, slot]),
                pltpu.make_async_copy(vpool_ref.at[layer, row], vbuf_ref.at[slot, r],
                                      sem_ref.at[1, slot]))

    def start_all(bb, slot):
        def body(r, c):
            ck, cv = copies(bb, r, slot)
            ck.start()
            cv.start()
            return c
        lax.fori_loop(0, topk, body, 0)

    def wait_all(bb, slot):
        def body(r, c):
            ck, cv = copies(bb, r, slot)
            ck.wait()
            cv.wait()
            return c
        lax.fori_loop(0, topk, body, 0)

    slot = b % 2

    @pl.when(b == 0)
    def _():
        start_all(b, slot)

    @pl.when(b + 1 < nb)
    def _():
        start_all(b + 1, 1 - slot)

    wait_all(b, slot)

    q = q_ref[0]
    in_past = idxv_ref[0] >= 0
    rep = A_HEADS // A_KV
    head_group = _iota((A_HEADS, A_DH), 0) // rep
    out = jnp.zeros((A_HEADS, A_DH), F32)
    for g in range(A_KV):
        kg = jnp.where(in_past, kbuf_ref[slot, :, g, :], knew_ref[0, g:g + 1, :]).astype(BF16)
        vg = jnp.where(in_past, vbuf_ref[slot, :, g, :], vnew_ref[0, g:g + 1, :]).astype(BF16)
        s = _dot_nt(q, kg) * (A_DH ** -0.5)
        m = jnp.max(s, axis=1, keepdims=True)
        p = jnp.exp(s - m)
        p = p / jnp.sum(p, axis=1, keepdims=True)
        o = _dot(p.astype(BF16), vg)
        out = jnp.where(head_group == g, o, out)
    o_ref[0] = out.astype(o_ref.dtype)


def _dsa_gather_attend(idx, q, k_new, v_new, pool_k, pool_v, *, layer):
    bd, topk = idx.shape[:2]
    grid_spec = pltpu.PrefetchScalarGridSpec(
        num_scalar_prefetch=1,
        grid=(bd,),
        in_specs=[pl.BlockSpec((1, A_HEADS, A_DH), lambda b, i: (b, 0, 0)),
                  pl.BlockSpec((1, topk, 1), lambda b, i: (b, 0, 0)),
                  pl.BlockSpec((1, A_KV, A_DH), lambda b, i: (b, 0, 0)),
                  pl.BlockSpec((1, A_KV, A_DH), lambda b, i: (b, 0, 0)),
                  pl.BlockSpec(memory_space=pl.ANY),
                  pl.BlockSpec(memory_space=pl.ANY)],
        out_specs=pl.BlockSpec((1, A_HEADS, A_DH), lambda b, i: (b, 0, 0)),
        scratch_shapes=[pltpu.VMEM((2, topk, A_KV, A_DH), F32),
                        pltpu.VMEM((2, topk, A_KV, A_DH), F32),
                        pltpu.SemaphoreType.DMA((2, 2))],
    )
    return pl.pallas_call(
        functools.partial(_dsa_gather_attend_kernel, layer=layer, topk=topk),
        out_shape=jax.ShapeDtypeStruct((bd, A_HEADS, A_DH), BF16),
        grid_spec=grid_spec,
        compiler_params=pltpu.CompilerParams(dimension_semantics=("arbitrary",),
                                             vmem_limit_bytes=VMEM_LIMIT, disable_bounds_checks=True),
        name="dsa_gather_attend",
    )(idx.reshape(bd, topk), q, idx, k_new, v_new,
      pool_k.reshape(pool_k.shape[0], -1, A_KV, A_DH), pool_v.reshape(pool_v.shape[0], -1, A_KV, A_DH))


def _head_rms_gate(o, gain, gate):
    ms = jnp.mean(o * o, axis=-1, keepdims=True)
    return o * lax.rsqrt(ms + EPS) * gain * _silu(gate)


def _gla_prompt_kernel(qkr_ref, v_ref, sm_ref, w2_ref, b2_ref, gn_ref, o_ref, sfin_ref, s_ref,
                       *, tt, chunk):
    i = pl.program_id(1)
    ni = pl.num_programs(1)

    @pl.when(i == 0)
    def _():
        s_ref[...] = jnp.zeros_like(s_ref)

    hdk = B_HEADS * B_DK
    r_i = _iota((chunk, chunk), 0)
    c_i = _iota((chunk, chunk), 1)
    causal = c_i <= r_i
    tril = causal.astype(BF16)
    x = _dot(sm_ref[...].astype(BF16), w2_ref[...]) + b2_ref[...]
    log_a = _log_sigmoid(x) * (1.0 / B_TAU)
    for c in range(tt // chunk):
        rows = slice(c * chunk, (c + 1) * chunk)
        for h in range(B_HEADS):
            dk = slice(h * B_DK, (h + 1) * B_DK)
            dv = slice(h * B_DV, (h + 1) * B_DV)
            q = qkr_ref[rows, dk] * (B_DK ** -0.5)
            k = qkr_ref[rows, hdk + h * B_DK:hdk + (h + 1) * B_DK]
            gate = qkr_ref[rows, 2 * hdk + h * B_DV:2 * hdk + (h + 1) * B_DV]
            v = v_ref[rows, dv]
            bc = _dot_exact_lhs(tril, log_a[rows, dk])
            bl = bc[chunk - 1:chunk, :]
            qd = (q * jnp.exp(bc)).astype(BF16)
            kd = (k * jnp.exp(-bc)).astype(BF16)
            att = jnp.where(causal, _dot_nt(qd, kd), 0.0)
            s_old = s_ref[h]
            o = _dot(qd, s_old.astype(BF16)) + _dot(att.astype(BF16), v)
            kdec = k * jnp.exp(bl - bc)
            decay = jnp.broadcast_to(jnp.exp(bl), (chunk, B_DK)).T
            decay = jnp.concatenate([decay] * (B_DV // B_DK), axis=1)
            s_ref[h] = decay * s_old + _dot(kdec.T.astype(BF16), v)
            o_ref[rows, dv] = _head_rms_gate(o, gn_ref[...], gate).astype(o_ref.dtype)

    @pl.when(i == ni - 1)
    def _():
        sfin_ref[0] = s_ref[...]


def _gla_prompt(qkr, v, small, w2pad, b2, gn, *, batch, seq, tt=256, chunk=128):
    nt = seq // tt
    hdk, hdv = B_HEADS * B_DK, B_HEADS * B_DV

    def rows(width):
        return pl.BlockSpec((tt, width), lambda b, i: (b * nt + i, 0))

    def const(shape):
        return pl.BlockSpec(shape, lambda b, i: (0,) * len(shape))

    return pl.pallas_call(
        functools.partial(_gla_prompt_kernel, tt=tt, chunk=chunk),
        out_shape=[jax.ShapeDtypeStruct((batch * seq, hdv), BF16),
                   jax.ShapeDtypeStruct((batch, B_HEADS, B_DK, B_DV), F32)],
        grid=(batch, nt),
        in_specs=[rows(2 * hdk + hdv), rows(hdv), rows(LANES),
                  const((LANES, hdk)), const((1, hdk)), const((1, B_DV))],
        out_specs=[rows(hdv),
                   pl.BlockSpec((1, B_HEADS, B_DK, B_DV), lambda b, i: (b, 0, 0, 0))],
        scratch_shapes=[pltpu.VMEM((B_HEADS, B_DK, B_DV), F32)],
        compiler_params=_cparams(("arbitrary", "arbitrary")),
        name="gla_prompt",
    )(qkr, v, small, w2pad, b2.reshape(1, hdk), gn.reshape(1, B_DV))


def _row_to_col(row):
    n = row.shape[1]
    eye = _iota((n, n), 0) == _iota((n, n), 1)
    return jnp.sum(jnp.where(eye, jnp.broadcast_to(row, (n, n)), 0.0), axis=1, keepdims=True)


def _gla_step_kernel(qkr_ref, v_ref, sm_ref, w2_ref, b2_ref, gn_ref, s_ref, o_ref, snew_ref):
    hdk = B_HEADS * B_DK
    smb = jnp.broadcast_to(sm_ref[0], (SUBLANES, LANES)).astype(BF16)
    x = _dot(smb, w2_ref[...])[0:1, :] + b2_ref[...]
    log_a = _log_sigmoid(x) * (1.0 / B_TAU)
    for h in range(B_HEADS):
        dk = slice(h * B_DK, (h + 1) * B_DK)
        dv = slice(h * B_DV, (h + 1) * B_DV)
        g_row = log_a[:, dk]
        q_row = qkr_ref[0, :, dk] * (B_DK ** -0.5)
        k_row = qkr_ref[0, :, hdk + h * B_DK:hdk + (h + 1) * B_DK]
        gate = qkr_ref[0, :, 2 * hdk + h * B_DV:2 * hdk + (h + 1) * B_DV]
        v_row = v_ref[0, :, dv].astype(F32)
        qd_row = q_row * jnp.exp(g_row)
        kd_row = k_row * jnp.exp(-g_row)
        s_old = s_ref[0, h]
        snew_ref[0, h] = _row_to_col(jnp.exp(g_row)) * s_old + _row_to_col(k_row) * v_row
        att = jnp.sum(qd_row * kd_row, axis=1, keepdims=True)
        o = jnp.sum(_row_to_col(qd_row) * s_old, axis=0, keepdims=True) + att * v_row
        o_ref[0, :, dv] = _head_rms_gate(o, gn_ref[...], gate).astype(o_ref.dtype)


def _gla_step(qkr, v, small, w2pad, b2, gn, state):
    bd = qkr.shape[0]
    hdk, hdv = B_HEADS * B_DK, B_HEADS * B_DV

    def row(width):
        return pl.BlockSpec((1, 1, width), lambda b: (b, 0, 0))

    def const(shape):
        return pl.BlockSpec(shape, lambda b: (0,) * len(shape))

    st = pl.BlockSpec((1, B_HEADS, B_DK, B_DV), lambda b: (b, 0, 0, 0))
    return pl.pallas_call(
        _gla_step_kernel,
        out_shape=[jax.ShapeDtypeStruct((bd, 1, hdv), BF16),
                   jax.ShapeDtypeStruct((bd, B_HEADS, B_DK, B_DV), F32)],
        grid=(bd,),
        in_specs=[row(2 * hdk + hdv), row(hdv), row(LANES),
                  const((LANES, hdk)), const((1, hdk)), const((1, B_DV)), st],
        out_specs=[row(hdv), st],
        compiler_params=_cparams(("arbitrary",)),
        name="gla_step",
    )(qkr.reshape(bd, 1, -1), v.reshape(bd, 1, -1), small.reshape(bd, 1, -1),
      w2pad, b2.reshape(1, hdk), gn.reshape(1, B_DV), state)


def _head_rms_sig(h, gain, gate):
    ms = jnp.mean(h * h, axis=-1, keepdims=True)
    return h * lax.rsqrt(ms + EPS) * gain * _sigmoid(gate)


def _mlstm_prompt_kernel(qko_ref, v_ref, sm_ref, bias_ref, hn_ref, o_ref, cfin_ref, nfin_ref, mfin_ref,
                         c_ref, n_ref, m_ref, *, tt, chunk):
    i = pl.program_id(1)
    ni = pl.num_programs(1)

    @pl.when(i == 0)
    def _():
        c_ref[...] = jnp.zeros_like(c_ref)
        n_ref[...] = jnp.zeros_like(n_ref)
        m_ref[...] = jnp.zeros_like(m_ref)

    hdk = C_HEADS * C_DK
    r_i = _iota((chunk, chunk), 0)
    c_i = _iota((chunk, chunk), 1)
    causal = c_i <= r_i
    tril = causal.astype(BF16)
    pre = sm_ref[...] + bias_ref[...]
    for c in range(tt // chunk):
        rows = slice(c * chunk, (c + 1) * chunk)
        for h in range(C_HEADS):
            dk = slice(h * C_DK, (h + 1) * C_DK)
            dv = slice(h * C_DV, (h + 1) * C_DV)
            icol = pre[rows, h:h + 1]
            fcol = _log_sigmoid(pre[rows, C_HEADS + h:C_HEADS + h + 1])
            bmat = _dot_exact_lhs(tril, jnp.broadcast_to(fcol, (chunk, chunk)))
            bcol = bmat[:, 0:1]
            b_t = bmat.T
            i_t = jnp.broadcast_to(icol, (chunk, chunk)).T
            m_prev = m_ref[h, 0:1, 0:1]
            log_d = jnp.where(causal, bmat - b_t + i_t, NEG)
            inter = bcol + m_prev
            m_t = jnp.maximum(inter, jnp.max(log_d, axis=1, keepdims=True))
            dmat = jnp.exp(log_d - m_t)
            g = jnp.exp(inter - m_t)
            q = qko_ref[rows, dk]
            ks = qko_ref[rows, hdk + h * C_DK:hdk + (h + 1) * C_DK] * (C_DK ** -0.5)
            gate = qko_ref[rows, 2 * hdk + h * C_DV:2 * hdk + (h + 1) * C_DV]
            v = v_ref[rows, dv]
            qb = q.astype(BF16)
            qk = _dot_nt(qb, ks.astype(BF16)) * dmat
            c_old = c_ref[h]
            n_old = n_ref[h, 0:1, :]
            num = g * _dot(qb, c_old.astype(BF16)) + _dot(qk.astype(BF16), v)
            den = g * jnp.sum(q * n_old, axis=1, keepdims=True) + jnp.sum(qk, axis=1, keepdims=True)
            hh = num / jnp.maximum(jnp.abs(den), jnp.exp(-m_t))
            o_ref[rows, dv] = _head_rms_sig(hh, hn_ref[...], gate).astype(o_ref.dtype)
            m_last = m_t[chunk - 1:chunk, :]
            g_last = g[chunk - 1:chunk, :]
            wcol = jnp.exp(bcol[chunk - 1:chunk, :] - bcol + icol - m_last)
            kw = ks * wcol
            c_ref[h] = g_last * c_old + _dot(kw.T.astype(BF16), v)
            n_ref[h] = jnp.broadcast_to(g_last * n_old + jnp.sum(kw, axis=0, keepdims=True),
                                        (SUBLANES, C_DK))
            m_ref[h] = jnp.broadcast_to(m_last, (SUBLANES, LANES))

    @pl.when(i == ni - 1)
    def _():
        cfin_ref[0] = c_ref[...]
        nfin_ref[0] = n_ref[...]
        mfin_ref[0] = m_ref[...]


def _mlstm_prompt(qko, v, small, bias, hn, *, batch, seq, tt=256, chunk=128):
    nt = seq // tt
    hdk, hdv = C_HEADS * C_DK, C_HEADS * C_DV

    def rows(width):
        return pl.BlockSpec((tt, width), lambda b, i: (b * nt + i, 0))

    def const(shape):
        return pl.BlockSpec(shape, lambda b, i: (0,) * len(shape))

    def per_batch(shape):
        return pl.BlockSpec((1,) + shape, lambda b, i: (b,) + (0,) * len(shape))

    return pl.pallas_call(
        functools.partial(_mlstm_prompt_kernel, tt=tt, chunk=chunk),
        out_shape=[jax.ShapeDtypeStruct((batch * seq, hdv), BF16),
                   jax.ShapeDtypeStruct((batch, C_HEADS, C_DK, C_DV), F32),
                   jax.ShapeDtypeStruct((batch, C_HEADS, SUBLANES, C_DK), F32),
                   jax.ShapeDtypeStruct((batch, C_HEADS, SUBLANES, LANES), F32)],
        grid=(batch, nt),
        in_specs=[rows(2 * hdk + hdv), rows(hdv), rows(LANES), const((1, LANES)), const((1, C_DV))],
        out_specs=[rows(hdv), per_batch((C_HEADS, C_DK, C_DV)),
                   per_batch((C_HEADS, SUBLANES, C_DK)), per_batch((C_HEADS, SUBLANES, LANES))],
        scratch_shapes=[pltpu.VMEM((C_HEADS, C_DK, C_DV), F32),
                        pltpu.VMEM((C_HEADS, SUBLANES, C_DK), F32),
                        pltpu.VMEM((C_HEADS, SUBLANES, LANES), F32)],
        compiler_params=_cparams(("arbitrary", "arbitrary")),
        name="mlstm_prompt",
    )(qko, v, small, bias, hn.reshape(1, C_DV))


def _mlstm_step_kernel(qko_ref, v_ref, sm_ref, bias_ref, hn_ref, c_ref, n_ref, m_ref,
                       o_ref, cnew_ref, nnew_ref, mnew_ref):
    hdk = C_HEADS * C_DK
    pre = sm_ref[0] + bias_ref[...]
    m_all = m_ref[0]
    m_out = jnp.zeros_like(m_all)
    for h in range(C_HEADS):
        dk = slice(h * C_DK, (h + 1) * C_DK)
        dv = slice(h * C_DV, (h + 1) * C_DV)
        i_pre = pre[:, h:h + 1]
        log_f = _log_sigmoid(pre[:, C_HEADS + h:C_HEADS + h + 1])
        m_prev = m_all[:, h:h + 1]
        inter = log_f + m_prev
        m_t = jnp.maximum(inter, i_pre)
        d = jnp.exp(i_pre - m_t)
        g = jnp.exp(inter - m_t)
        q_row = qko_ref[0, :, dk]
        ks_row = qko_ref[0, :, hdk + h * C_DK:hdk + (h + 1) * C_DK] * (C_DK ** -0.5)
        gate = qko_ref[0, :, 2 * hdk + h * C_DV:2 * hdk + (h + 1) * C_DV]
        v_row = v_ref[0, :, dv].astype(F32)
        c_old = c_ref[0, h]
        n_old = n_ref[0, h:h + 1, :]
        qk = jnp.sum(q_row * ks_row, axis=1, keepdims=True) * d
        num = g * jnp.sum(_row_to_col(q_row) * c_old, axis=0, keepdims=True) + qk * v_row
        den = g * jnp.sum(q_row * n_old, axis=1, keepdims=True) + qk
        hh = num / jnp.maximum(jnp.abs(den), jnp.exp(-m_t))
        o_ref[0, :, dv] = _head_rms_sig(hh, hn_ref[...], gate).astype(o_ref.dtype)
        cnew_ref[0, h] = g * c_old + _row_to_col(d * ks_row) * v_row
        nnew_ref[0, h:h + 1, :] = g * n_old + d * ks_row
        m_out = jnp.where(_iota(m_all.shape, 1) == h, m_t, m_out)
    mnew_ref[0] = m_out


def _mlstm_step(qko, v, small, bias, hn, c0, n0, m0):
    bd = qko.shape[0]
    hdk, hdv = C_HEADS * C_DK, C_HEADS * C_DV

    def row(width):
        return pl.BlockSpec((1, 1, width), lambda b: (b, 0, 0))

    def const(shape):
        return pl.BlockSpec(shape, lambda b: (0,) * len(shape))

    cs = pl.BlockSpec((1, C_HEADS, C_DK, C_DV), lambda b: (b, 0, 0, 0))
    ns = pl.BlockSpec((1, C_HEADS, C_DK), lambda b: (b, 0, 0))
    return pl.pallas_call(
        _mlstm_step_kernel,
        out_shape=[jax.ShapeDtypeStruct((bd, 1, hdv), BF16),
                   jax.ShapeDtypeStruct((bd, C_HEADS, C_DK, C_DV), F32),
                   jax.ShapeDtypeStruct((bd, C_HEADS, C_DK), F32),
                   jax.ShapeDtypeStruct((bd, 1, C_HEADS), F32)],
        grid=(bd,),
        in_specs=[row(2 * hdk + hdv), row(hdv), row(LANES), const((1, LANES)), const((1, C_DV)),
                  cs, ns, row(C_HEADS)],
        out_specs=[row(hdv), cs, ns, row(C_HEADS)],
        compiler_params=_cparams(("arbitrary",)),
        name="mlstm_step",
    )(qko.reshape(bd, 1, -1), v.reshape(bd, 1, -1), small.reshape(bd, 1, -1), bias,
      hn.reshape(1, C_DV), c0, n0, m0.reshape(bd, 1, C_HEADS))


def _swa_prompt_kernel(sink_ref, q_ref, kvp_ref, kvc_ref, o_ref, *, w):
    j = pl.program_id(1)
    band = jnp.concatenate([kvp_ref[...], kvc_ref[...]], axis=0)
    kk = band[:, :LANES]
    vv = band[:, LANES:]
    lane = _iota((2 * w, LANES), 1)
    qi = _iota((w, 2 * w), 0)
    kj = _iota((w, 2 * w), 1)
    allowed = jnp.logical_and(kj >= qi, kj <= qi + w)
    allowed = jnp.logical_and(allowed, jnp.logical_or(j > 0, kj >= w))
    rep = D_HEADS // D_KV
    pairs = rep // 2
    allowed = jnp.concatenate([allowed] * pairs, axis=0)
    zero = jnp.zeros_like(kk)
    for g in range(D_KV):
        own = (lane // D_DH) == g
        k_own = jnp.where(own, kk, zero)
        v_own = jnp.where(own, vv, zero)
        k_oth = pltpu.roll(k_own, D_DH, 1)
        v_oth = pltpu.roll(v_own, D_DH, 1)
        k_lo, k_hi = (k_own, k_oth) if g == 0 else (k_oth, k_own)
        v_lo, v_hi = (v_own, v_oth) if g == 0 else (v_oth, v_own)
        qs = jnp.concatenate([q_ref[:, (g * pairs + p) * LANES:(g * pairs + p + 1) * LANES]
                              for p in range(pairs)], axis=0)
        outs = []
        for half, (kh, vh) in enumerate(((k_lo, v_lo), (k_hi, v_hi))):
            kh = kh.astype(BF16)
            vh = vh.astype(BF16)
            s = _dot_nt(qs, kh) * (D_DH ** -0.5)
            s = jnp.where(allowed, s, NEG)
            sink = jnp.concatenate(
                [jnp.full((w, 1), sink_ref[2 * (g * pairs + p) + half], F32) for p in range(pairs)],
                axis=0)
            m = jnp.maximum(jnp.max(s, axis=1, keepdims=True), sink)
            e = jnp.where(allowed, jnp.exp(s - m), 0.0)
            p_ = e / (jnp.sum(e, axis=1, keepdims=True) + jnp.exp(sink - m))
            outs.append(_dot(p_.astype(BF16), vh))
        o = outs[0] + outs[1]
        for p in range(pairs):
            o_ref[:, (g * pairs + p) * LANES:(g * pairs + p + 1) * LANES] = (
                o[p * w:(p + 1) * w].astype(o_ref.dtype))


def _swa_prompt(q, kv, sinks, *, batch, seq):
    w = WINDOW
    nb = seq // w
    grid_spec = pltpu.PrefetchScalarGridSpec(
        num_scalar_prefetch=0,
        grid=(batch, nb),
        in_specs=[pl.BlockSpec(memory_space=pltpu.SMEM),
                  pl.BlockSpec((w, D_HEADS * D_DH), lambda b, j: (b * nb + j, 0)),
                  pl.BlockSpec((w, 2 * LANES), lambda b, j: (b * nb + jnp.maximum(j - 1, 0), 0)),
                  pl.BlockSpec((w, 2 * LANES), lambda b, j: (b * nb + j, 0))],
        out_specs=pl.BlockSpec((w, D_HEADS * D_DH), lambda b, j: (b * nb + j, 0)),
    )
    return pl.pallas_call(
        functools.partial(_swa_prompt_kernel, w=w),
        out_shape=jax.ShapeDtypeStruct((batch * seq, D_HEADS * D_DH), BF16),
        grid_spec=grid_spec,
        compiler_params=_cparams(("arbitrary", "arbitrary")),
        name="swa_prompt",
    )(sinks, q, kv, kv)


def _swa_step_kernel(q_ref, kvn_ref, bk_ref, bv_ref, sink_ref, o_ref, nk_ref, nv_ref):
    w = bk_ref.shape[1]
    rep = D_HEADS // D_KV
    q = q_ref[0]
    dup = (_iota((D_DH, LANES), 1) % D_DH == _iota((D_DH, LANES), 0)).astype(BF16)
    q2 = _dot(q, dup)
    own = (_iota((D_HEADS, LANES), 1) // D_DH) == (_iota((D_HEADS, LANES), 0) // rep)
    qm = jnp.where(own, q2, 0.0)
    bk = bk_ref[0]
    bv = bv_ref[0]
    k_new = kvn_ref[0, :, :LANES]
    v_new = kvn_ref[0, :, LANES:]
    s = _dot_nt(qm.astype(BF16), bk.astype(BF16)) * (D_DH ** -0.5)
    s_new = jnp.sum(qm * k_new.astype(BF16).astype(F32), axis=1, keepdims=True) * (D_DH ** -0.5)
    sink = sink_ref[...]
    m = jnp.maximum(jnp.maximum(jnp.max(s, axis=1, keepdims=True), s_new), sink)
    e = jnp.exp(s - m)
    e_new = jnp.exp(s_new - m)
    den = jnp.sum(e, axis=1, keepdims=True) + e_new + jnp.exp(sink - m)
    o = _dot((e / den).astype(BF16), bv.astype(BF16))
    o = o + (e_new / den).astype(BF16).astype(F32) * v_new.astype(BF16).astype(F32)
    o = jnp.where(own, o, 0.0)
    o = o + pltpu.roll(o, D_DH, 1)
    o_ref[0] = o[:, :D_DH].astype(o_ref.dtype)
    nk_ref[0, 0:w - 1, :] = bk_ref[0, 1:w, :]
    nk_ref[0, w - 1:w, :] = k_new
    nv_ref[0, 0:w - 1, :] = bv_ref[0, 1:w, :]
    nv_ref[0, w - 1:w, :] = v_new


def _swa_step(q, kv_new, buf_k, buf_v, sinks):
    bd = q.shape[0]
    w = buf_k.shape[1]
    buf = pl.BlockSpec((1, w, LANES), lambda b: (b, 0, 0))
    return pl.pallas_call(
        _swa_step_kernel,
        out_shape=[jax.ShapeDtypeStruct((bd, D_HEADS, D_DH), BF16),
                   jax.ShapeDtypeStruct((bd, w, LANES), F32),
                   jax.ShapeDtypeStruct((bd, w, LANES), F32)],
        grid=(bd,),
        in_specs=[pl.BlockSpec((1, D_HEADS, D_DH), lambda b: (b, 0, 0)),
                  pl.BlockSpec((1, 1, 2 * LANES), lambda b: (b, 0, 0)),
                  buf, buf,
                  pl.BlockSpec((D_HEADS, 1), lambda b: (0, 0))],
        out_specs=[pl.BlockSpec((1, D_HEADS, D_DH), lambda b: (b, 0, 0)), buf, buf],
        compiler_params=_cparams(("arbitrary",)),
        name="swa_step",
    )(q.reshape(bd, D_HEADS, D_DH), kv_new.reshape(bd, 1, 2 * LANES),
      buf_k.reshape(bd, w, LANES), buf_v.reshape(bd, w, LANES), sinks.reshape(D_HEADS, 1))


def _even_weights(w, a_qn, a_kn, a_kin, b_w2):
    o = [0]
    for s in (A_HEADS * A_DH, A_KV * A_DH, A_KV * A_DH, IDX_HEADS * IDX_DIM, IDX_DIM, IDX_HEADS,
              B_HEADS * B_DK, B_HEADS * B_DK, B_HEADS * B_DV, B_RANK, B_HEADS * B_DV):
        o.append(o[-1] + s)
    d = w.shape[0]
    col = lambda i: w[:, o[i]:o[i + 1]]
    pad = LANES - (IDX_DIM + IDX_HEADS + B_RANK)
    return dict(
        qa=col(0).astype(BF16),
        ka=col(1).astype(BF16),
        va=col(2).astype(BF16),
        qi_vb=jnp.concatenate([col(3) * (IDX_DIM ** -0.5), col(8)], axis=1).astype(BF16),
        small=jnp.concatenate([col(4), col(5), col(9), jnp.zeros((d, pad), w.dtype)], axis=1).astype(BF16),
        qkr=jnp.concatenate([col(6), col(7), col(10)], axis=1).astype(BF16),
        qa_gain=jnp.tile(a_qn, A_HEADS),
        ka_gain=jnp.tile(a_kn, A_KV),
        small_gain=jnp.concatenate([a_kin, jnp.zeros((LANES - IDX_DIM,), F32)]),
        w2pad=jnp.zeros((LANES, B_HEADS * B_DK), F32).at[
            IDX_DIM + IDX_HEADS:IDX_DIM + IDX_HEADS + B_RANK].set(b_w2).astype(BF16),
    )


def _odd_weights(w, c_bi, c_bf, d_qn, d_kn):
    o = [0]
    for s in (C_HEADS * C_DK, C_HEADS * C_DK, C_HEADS * C_DV, C_HEADS, C_HEADS, C_HEADS * C_DV,
              D_HEADS * D_DH, D_KV * D_DH, D_KV * D_DH):
        o.append(o[-1] + s)
    d = w.shape[0]
    col = lambda i: w[:, o[i]:o[i + 1]]
    return dict(
        qko=jnp.concatenate([col(0), col(1), col(5)], axis=1).astype(BF16),
        vc=col(2).astype(BF16),
        small=jnp.concatenate([col(3), col(4), jnp.zeros((d, LANES - 2 * C_HEADS), w.dtype)],
                              axis=1).astype(BF16),
        qd=col(6).astype(BF16),
        kvd=jnp.concatenate([col(7), col(8)], axis=1).astype(BF16),
        qd_gain=jnp.tile(d_qn, D_HEADS),
        kd_gain=jnp.concatenate([jnp.tile(d_kn, D_KV), jnp.ones((LANES,), F32)]),
        bias=jnp.concatenate([c_bi, c_bf, jnp.zeros((LANES - 2 * C_HEADS,), F32)]).reshape(1, LANES),
    )


def _even_in(h, ew):
    q, = _proj(h, ew["qa"], "rms128", (BF16,), ew["qa_gain"])
    k32, k16 = _proj(h, ew["ka"], "rms128", (F32, BF16), ew["ka_gain"])
    v32, v16 = _proj(h, ew["va"], "raw", (F32, BF16))
    qi_vb, = _proj(h, ew["qi_vb"], "raw", (BF16,))
    small, kia, kib = _proj(h, ew["small"], "small_even", (F32, BF16, BF16), ew["small_gain"])
    qkr, = _proj(h, ew["qkr"], "raw", (F32,))
    n_qi = IDX_HEADS * IDX_DIM
    return dict(q=q, k32=k32, k16=k16, v32=v32, v16=v16, qi=qi_vb[:, :n_qi], vb=qi_vb[:, n_qi:],
                small=small, kia=kia, kib=kib, qkr=qkr)


def _odd_in(h, ow):
    qko, = _proj(h, ow["qko"], "raw", (F32,))
    vc, = _proj(h, ow["vc"], "raw", (BF16,))
    small, = _proj(h, ow["small"], "raw", (F32,))
    qd, = _proj(h, ow["qd"], "rms64", (BF16,), ow["qd_gain"])
    kv32, = _proj(h, ow["kvd"], "kd_vd", (F32,), ow["kd_gain"])
    return dict(qko=qko, vc=vc, small=small, qd=qd, kv32=kv32)


def kernel(x_prompt, x_sample, cache_a_k, cache_a_v, cache_a_kidx, state_b_s, state_c_c, state_c_n,
           state_c_m, cache_d_k, cache_d_v, state_ffn_conv, page_table, norm_mix, norm_ffn,
           even_w_in, even_w_out, a_q_norm, a_k_norm, a_kidx_norm, b_gate_w2, b_gate_b, b_head_norm,
           odd_w_in, odd_w_out, c_i_bias, c_f_bias, c_head_norm, d_q_norm, d_k_norm, d_sinks,
           ffn_w_up, ffn_conv_w, ffn_conv_b, ffn_w_down):
    bp, seq, d_model = x_prompt.shape
    bd = x_sample.shape[0]
    depth = norm_mix.shape[0]
    d_ff = ffn_w_down.shape[1]
    mp = bp * seq
    xp = x_prompt.reshape(mp, d_model)
    xs = x_sample.reshape(bd, d_model)
    hp = _rms_cast(xp, norm_mix[0], tm=512)
    hs = _rms_cast(xs, norm_mix[0], tm=bd)
    pool_ki_t = jnp.swapaxes(cache_a_kidx, 2, 3)

    p_out = {k: [] for k in ("ak", "av", "aki", "bs", "cc", "cn", "cm", "dk", "dv", "conv")}
    s_out = {k: [] for k in p_out}
    for layer in range(depth):
        if layer % 2 == 0:
            e = layer // 2
            ew = _even_weights(even_w_in[e], a_q_norm[e], a_k_norm[e], a_kidx_norm[e], b_gate_w2[e])
            w_out = even_w_out[e].astype(BF16)
            z = _even_in(hp, ew)
            a_o = _dsa_prompt(z["q"], z["k16"], z["v16"], z["qi"], z["kia"], z["kib"], z["small"],
                              batch=bp, seq=seq)
            g_o, s_fin = _gla_prompt(z["qkr"], z["vb"], z["small"], ew["w2pad"], b_gate_b[e],
                                     b_head_norm[e], batch=bp, seq=seq)
            p_out["ak"].append(z["k32"].reshape(bp, seq, A_KV, A_DH))
            p_out["av"].append(z["v32"].reshape(bp, seq, A_KV, A_DH))
            p_out["aki"].append(z["small"][:, :IDX_DIM].reshape(bp, seq, IDX_DIM))
            p_out["bs"].append(s_fin)
            mix_p = (a_o, g_o)
            z = _even_in(hs, ew)
            small = z["small"]
            idx = _dsa_select(page_table,
                              z["qi"].reshape(bd, IDX_HEADS, IDX_DIM),
                              small[:, IDX_DIM:IDX_DIM + IDX_HEADS].reshape(bd, IDX_HEADS, 1),
                              small[:, :IDX_DIM].reshape(bd, 1, IDX_DIM),
                              pool_ki_t, layer=e)
            a_o = _dsa_gather_attend(idx, z["q"].reshape(bd, A_HEADS, A_DH),
                                     z["k32"].reshape(bd, A_KV, A_DH), z["v32"].reshape(bd, A_KV, A_DH),
                                     cache_a_k, cache_a_v, layer=e)
            g_o, s_new = _gla_step(z["qkr"], z["vb"], small, ew["w2pad"], b_gate_b[e],
                                   b_head_norm[e], state_b_s[e])
            s_out["ak"].append(z["k32"].reshape(bd, 1, A_KV, A_DH))
            s_out["av"].append(z["v32"].reshape(bd, 1, A_KV, A_DH))
            s_out["aki"].append(small[:, :IDX_DIM].reshape(bd, 1, IDX_DIM))
            s_out["bs"].append(s_new)
            mix_s = (a_o.reshape(bd, A_HEADS * A_DH), g_o.reshape(bd, B_HEADS * B_DV))
        else:
            o = layer // 2
            ow = _odd_weights(odd_w_in[o], c_i_bias[o], c_f_bias[o], d_q_norm[o], d_k_norm[o])
            w_out = odd_w_out[o].astype(BF16)
            z = _odd_in(hp, ow)
            c_h, cf, nf, mf = _mlstm_prompt(z["qko"], z["vc"], z["small"], ow["bias"], c_head_norm[o],
                                            batch=bp, seq=seq)
            d_o = _swa_prompt(z["qd"], z["kv32"], d_sinks[o], batch=bp, seq=seq)
            kv = z["kv32"].reshape(bp, seq, 2, D_KV, D_DH)[:, seq - WINDOW:]
            p_out["cc"].append(cf)
            p_out["cn"].append(nf[:, :, 0, :])
            p_out["cm"].append(mf[:, :, 0, 0])
            p_out["dk"].append(kv[:, :, 0])
            p_out["dv"].append(kv[:, :, 1])
            mix_p = (c_h, d_o)
            z = _odd_in(hs, ow)
            c_h, c_new, n_new, m_new = _mlstm_step(z["qko"], z["vc"], z["small"], ow["bias"],
                                                   c_head_norm[o], state_c_c[o], state_c_n[o],
                                                   state_c_m[o])
            d_o, nbk, nbv = _swa_step(z["qd"], z["kv32"], cache_d_k[o], cache_d_v[o], d_sinks[o])
            s_out["cc"].append(c_new)
            s_out["cn"].append(n_new)
            s_out["cm"].append(m_new.reshape(bd, C_HEADS))
            s_out["dk"].append(nbk.reshape(bd, WINDOW, D_KV, D_DH))
            s_out["dv"].append(nbv.reshape(bd, WINDOW, D_KV, D_DH))
            mix_s = (c_h.reshape(bd, C_HEADS * C_DV), d_o.reshape(bd, D_HEADS * D_DH))

        xp, hfp = _outproj(mix_p[0], mix_p[1], w_out, xp, norm_ffn[layer])
        xs, hfs = _outproj(mix_s[0], mix_s[1], w_out, xs, norm_ffn[layer])

        wu = ffn_w_up[layer, :, :d_ff].astype(BF16)
        wg = ffn_w_up[layer, :, d_ff:].astype(BF16)
        wd = ffn_w_down[layer].astype(BF16)
        xp, glp = _ffn(xp, hfp, wu, wg, ffn_conv_w[layer], ffn_conv_b[layer], wd,
                       jnp.zeros((bp, SUBLANES, d_ff), F32), seq_len=seq, stepwise=False)
        xs, gls = _ffn(xs, hfs, wu, wg, ffn_conv_w[layer], ffn_conv_b[layer], wd,
                       state_ffn_conv[layer].swapaxes(0, 1), seq_len=1, stepwise=True)
        if layer + 1 < depth:
            hp = _rms_cast(xp, norm_mix[layer + 1], tm=512)
            hs = _rms_cast(xs, norm_mix[layer + 1], tm=bd)
        tiles = glp.shape[0] // bp
        p_out["conv"].append(glp.reshape(bp, tiles, SUBLANES, d_ff)[:, -1, SUBLANES - (CONV_W - 1):])
        s_out["conv"].append(jnp.stack([state_ffn_conv[layer][:, 1], gls], axis=1))

    order = ("ak", "av", "aki", "bs", "cc", "cn", "cm", "dk", "dv", "conv")
    return (xp.reshape(bp, seq, d_model), xs.reshape(bd, 1, d_model),
            *[jnp.stack(p_out[k]) for k in order],
            *[jnp.stack(s_out[k]) for k in order])
```

```python
import functools

import jax
import jax.numpy as jnp
from jax import lax
from jax.experimental import pallas as pl
from jax.experimental.pallas import tpu as pltpu

F32 = jnp.float32
BF16 = jnp.bfloat16
I32 = jnp.int32

EPS = 1e-6
NEG = -1e30

PAGE = 128
A_HEADS, A_KV, A_DH = 8, 4, 128
IDX_HEADS, IDX_DIM = 16, 64
TOPK = 256
B_HEADS, B_DK, B_DV, B_RANK, B_TAU = 4, 128, 256, 16, 16.0
C_HEADS, C_DK, C_DV = 4, 128, 256
D_HEADS, D_KV, D_DH = 16, 2, 64
WINDOW = 128
CONV_W = 3

VMEM_LIMIT = 56 * 1024 * 1024
LANES = 128
SUBLANES = 8


def _cparams(sem):
    return pltpu.CompilerParams(dimension_semantics=sem, vmem_limit_bytes=VMEM_LIMIT)


def _dot(a, b):
    return jnp.dot(a, b, preferred_element_type=F32)


def _dot_nt(a, b):
    return lax.dot_general(a, b, (((1,), (1,)), ((), ())), preferred_element_type=F32)


def _split3(x):
    hi = x.astype(BF16)
    r1 = x - hi.astype(F32)
    mid = r1.astype(BF16)
    lo = (r1 - mid.astype(F32)).astype(BF16)
    return hi, mid, lo


def _dot_exact_lhs(a_bf16, x):
    hi, mid, lo = _split3(x)
    return _dot(a_bf16, hi) + _dot(a_bf16, mid) + _dot(a_bf16, lo)


def _log_sigmoid(x):
    return jnp.minimum(x, 0.0) - jnp.log(1.0 + jnp.exp(-jnp.abs(x)))


def _sigmoid(x):
    return 1.0 / (1.0 + jnp.exp(-x))


def _silu(x):
    return x * _sigmoid(x)


def _iota(shape, dim):
    return lax.broadcasted_iota(I32, shape, dim)


def _rms_cast_kernel(x_ref, g_ref, o_ref):
    x = x_ref[...]
    ms = jnp.mean(x * x, axis=-1, keepdims=True)
    o_ref[...] = (x * lax.rsqrt(ms + EPS) * g_ref[...]).astype(o_ref.dtype)


def _rms_cast(x, gain, tm):
    m, d = x.shape
    return pl.pallas_call(
        _rms_cast_kernel,
        out_shape=jax.ShapeDtypeStruct((m, d), BF16),
        grid=(m // tm,),
        in_specs=[pl.BlockSpec((tm, d), lambda i: (i, 0)),
                  pl.BlockSpec((1, d), lambda i: (0, 0))],
        out_specs=pl.BlockSpec((tm, d), lambda i: (i, 0)),
        compiler_params=_cparams(("arbitrary",)),
        name="rms_cast",
    )(x, gain.reshape(1, d))


def _group_rms(acc, gain, width):
    tn = acc.shape[1]
    outs = []
    for c in range(tn // LANES):
        blk = acc[:, c * LANES:(c + 1) * LANES]
        sq = blk * blk
        if width == LANES:
            ms = jnp.sum(sq, axis=-1, keepdims=True) * (1.0 / width)
        else:
            low = _iota(blk.shape, 1) < width
            s_lo = jnp.sum(jnp.where(low, sq, 0.0), axis=-1, keepdims=True)
            s_hi = jnp.sum(jnp.where(low, 0.0, sq), axis=-1, keepdims=True)
            ms = jnp.where(low, s_lo, s_hi) * (1.0 / width)
        outs.append(blk * lax.rsqrt(ms + EPS) * gain[:, c * LANES:(c + 1) * LANES])
    return outs[0] if len(outs) == 1 else jnp.concatenate(outs, axis=1)


def _proj_kernel(h_ref, w_ref, *rest, mode, n_out, scale):
    acc = _dot(h_ref[...], w_ref[...].astype(BF16))
    outs = rest[-n_out:]
    if mode == "raw":
        y = acc if scale is None else acc * scale
    elif mode == "rms128":
        y = _group_rms(acc, rest[0][...], 128)
    elif mode == "rms64":
        y = _group_rms(acc, rest[0][...], 64)
    elif mode == "kd_vd":
        y = jnp.concatenate([_group_rms(acc[:, :LANES], rest[0][:, :LANES], 64), acc[:, LANES:]], axis=1)
    elif mode == "small_even":
        low = _iota(acc.shape, 1) < IDX_DIM
        ms = jnp.sum(jnp.where(low, acc * acc, 0.0), axis=-1, keepdims=True) * (1.0 / IDX_DIM)
        kin = acc * lax.rsqrt(ms + EPS) * rest[0][...]
        y = jnp.where(low, kin, acc)
        ka = jnp.where(low, kin, 0.0)
        outs[1][...] = ka.astype(BF16)
        outs[2][...] = pltpu.roll(ka, IDX_DIM, 1).astype(BF16)
        outs[0][...] = y
        return
    else:
        raise ValueError(mode)
    for o in outs:
        o[...] = y.astype(o.dtype)


def _proj(h, w, mode, out_dtypes, gain=None, tm=1024, tn=512, window=None, scale=None):
    m, k = h.shape
    if window is None:
        n = w.shape[1]
        tn = min(tn, n)
        w_spec = pl.BlockSpec((k, tn), lambda i, j: (0, j))
    else:
        layer, col0, n = window
        tn = min(tn, n)
        assert col0 % tn == 0
        w_spec = pl.BlockSpec((None, k, tn), lambda i, j: (layer, 0, col0 // tn + j))
    tm = min(tm, m)
    in_specs = [pl.BlockSpec((tm, k), lambda i, j: (i, 0)), w_spec]
    args = [h, w]
    if gain is not None:
        in_specs.append(pl.BlockSpec((1, tn), lambda i, j: (0, j)))
        args.append(gain.reshape(1, n).astype(F32))
    outs = pl.pallas_call(
        functools.partial(_proj_kernel, mode=mode, n_out=len(out_dtypes), scale=scale),
        out_shape=[jax.ShapeDtypeStruct((m, n), dt) for dt in out_dtypes],
        grid=(m // tm, n // tn),
        in_specs=in_specs,
        out_specs=[pl.BlockSpec((tm, tn), lambda i, j: (i, j)) for _ in out_dtypes],
        compiler_params=_cparams(("arbitrary", "arbitrary")),
        name="proj_" + mode,
    )(*args)
    return outs


def _outproj_kernel(a_ref, b_ref, wa_ref, wb_ref, x_ref, g_ref, xo_ref, ho_ref):
    y = x_ref[...] + _dot(a_ref[...], wa_ref[...]) + _dot(b_ref[...], wb_ref[...])
    xo_ref[...] = y
    ms = jnp.mean(y * y, axis=-1, keepdims=True)
    ho_ref[...] = (y * lax.rsqrt(ms + EPS) * g_ref[...]).astype(ho_ref.dtype)


def _outproj(a, b, w, x, gain, tm=512):
    m, d = x.shape
    ka, kb = a.shape[1], b.shape[1]
    tm = min(tm, m)
    return pl.pallas_call(
        _outproj_kernel,
        out_shape=[jax.ShapeDtypeStruct((m, d), F32), jax.ShapeDtypeStruct((m, d), BF16)],
        grid=(m // tm,),
        in_specs=[pl.BlockSpec((tm, ka), lambda i: (i, 0)),
                  pl.BlockSpec((tm, kb), lambda i: (i, 0)),
                  pl.BlockSpec((ka, d), lambda i: (0, 0)),
                  pl.BlockSpec((kb, d), lambda i: (0, 0)),
                  pl.BlockSpec((tm, d), lambda i: (i, 0)),
                  pl.BlockSpec((1, d), lambda i: (0, 0))],
        out_specs=[pl.BlockSpec((tm, d), lambda i: (i, 0)),
                   pl.BlockSpec((tm, d), lambda i: (i, 0))],
        compiler_params=_cparams(("arbitrary",)),
        name="outproj",
    )(a, b, w[:ka], w[ka:], x, gain.reshape(1, d))


def _ffn_up_kernel(h_ref, wu_ref, wg_ref, cw_ref, cb_ref, st_ref, a_ref, gl_ref, gbuf_ref, carry_ref,
                   *, tiles_per_seq, stepwise):
    i = pl.program_id(0)
    f = pl.program_id(1)
    tm = h_ref.shape[0]
    wu = wu_ref[...].astype(BF16)
    wg = wg_ref[...].astype(BF16)
    cw = cw_ref[...]
    cb = cb_ref[...]
    if stepwise:
        h = h_ref[...]
        u = _dot(h, wu)
        g = _dot(h, wg)
        gl_ref[...] = g
        gc = cb + cw[0:1, :] * st_ref[0] + cw[1:2, :] * st_ref[1] + cw[2:3, :] * g
        a_ref[...] = (_silu(gc) * u).astype(a_ref.dtype)
        return

    first = (i % tiles_per_seq) == 0

    @pl.when(first)
    def _():
        gbuf_ref[0:SUBLANES, :] = st_ref[0]

    @pl.when(jnp.logical_not(first))
    def _():
        gbuf_ref[0:SUBLANES, :] = carry_ref[f]

    sub = min(tm, 256)
    for r0 in range(0, tm, sub):
        h = h_ref[r0:r0 + sub, :]
        u = _dot(h, wu)
        g = _dot(h, wg)
        base = SUBLANES + r0
        gbuf_ref[base:base + sub, :] = g
        gm1 = gbuf_ref[base - 1:base - 1 + sub, :]
        gm2 = gbuf_ref[base - 2:base - 2 + sub, :]
        gc = cb + cw[0:1, :] * gm2 + cw[1:2, :] * gm1 + cw[2:3, :] * g
        a_ref[r0:r0 + sub, :] = (_silu(gc) * u).astype(a_ref.dtype)
    tail = gbuf_ref[tm:tm + SUBLANES, :]
    carry_ref[f] = tail
    gl_ref[0] = tail


def _ffn_down_kernel(a_ref, wd_ref, x_ref, xo_ref):
    xo_ref[...] = x_ref[...] + _dot(a_ref[...], wd_ref[...].astype(BF16))


def _ffn(x, h, w_up, w_down, layer, cw, cb, state, *, seq_len, stepwise, tm=1024, tf=512, tn=256):
    m, d = x.shape
    ff = w_down.shape[1]
    tm = min(tm, m)
    tf = min(tf, ff)
    tn = min(tn, d)
    n_i, n_f = m // tm, ff // tf
    tiles_per_seq = max(seq_len // tm, 1)
    if stepwise:
        st_spec = pl.BlockSpec((2, tm, tf), lambda i, f: (0, i, f))
        gl_shape = jax.ShapeDtypeStruct((m, ff), F32)
        gl_spec = pl.BlockSpec((tm, tf), lambda i, f: (i, f))
    else:
        st_spec = pl.BlockSpec((1, SUBLANES, tf), lambda i, f: (i // tiles_per_seq, 0, f))
        gl_shape = jax.ShapeDtypeStruct((n_i, SUBLANES, ff), F32)
        gl_spec = pl.BlockSpec((1, SUBLANES, tf), lambda i, f: (i, 0, f))
    a, gl = pl.pallas_call(
        functools.partial(_ffn_up_kernel, tiles_per_seq=tiles_per_seq, stepwise=stepwise),
        out_shape=[jax.ShapeDtypeStruct((m, ff), BF16), gl_shape],
        grid=(n_i, n_f),
        in_specs=[pl.BlockSpec((tm, d), lambda i, f: (i, 0)),
                  pl.BlockSpec((None, d, tf), lambda i, f: (layer, 0, f)),
                  pl.BlockSpec((None, d, tf), lambda i, f: (layer, 0, n_f + f)),
                  pl.BlockSpec((CONV_W, tf), lambda i, f: (0, f)),
                  pl.BlockSpec((1, tf), lambda i, f: (0, f)),
                  st_spec],
        out_specs=[pl.BlockSpec((tm, tf), lambda i, f: (i, f)), gl_spec],
        scratch_shapes=[pltpu.VMEM((SUBLANES + tm, tf), F32),
                        pltpu.VMEM((n_f, SUBLANES, tf), F32)],
        compiler_params=_cparams(("arbitrary", "arbitrary")),
        name="ffn_up_step" if stepwise else "ffn_up",
    )(h, w_up, w_up, cw, cb.reshape(1, ff), state)
    xo = pl.pallas_call(
        _ffn_down_kernel,
        out_shape=jax.ShapeDtypeStruct((m, d), F32),
        grid=(n_i, d // tn),
        in_specs=[pl.BlockSpec((tm, ff), lambda i, n: (i, 0)),
                  pl.BlockSpec((None, ff, tn), lambda i, n: (layer, 0, n)),
                  pl.BlockSpec((tm, tn), lambda i, n: (i, n))],
        out_specs=pl.BlockSpec((tm, tn), lambda i, n: (i, n)),
        compiler_params=_cparams(("arbitrary", "arbitrary")),
        name="ffn_down",
    )(a, w_down, x)
    return xo, gl


def _bisect_threshold(count_ge, lo, hi, k, max_iter=80):
    def cond(c):
        it, _, _, done = c
        return jnp.logical_and(it < max_iter, jnp.min(done) == 0)

    def body(c):
        it, lo, hi, done = c
        mid = 0.5 * (lo + hi)
        cnt = count_ge(mid)
        ge = cnt >= k
        stuck = jnp.logical_or(mid <= lo, mid >= hi)
        new_done = jnp.logical_or(jnp.logical_and(ge, cnt == k), stuck)
        lo = jnp.where(ge, mid, lo)
        hi = jnp.where(ge, hi, mid)
        return it + 1, lo, hi, jnp.maximum(done, new_done.astype(I32))

    done0 = (count_ge(lo) == k).astype(I32)
    _, lo, _, _ = lax.while_loop(cond, body, (jnp.int32(0), lo, hi, done0))
    return lo


def _dsa_prompt_kernel(q_ref, k_ref, v_ref, qi_ref, kia_ref, kib_ref, sm_ref, o_ref, sc_ref,
                       *, tq, ts, topk):
    i = pl.program_id(1)
    q0 = i * tq
    nk = (q0 + tq + ts - 1) // ts
    qpos = q0 + _iota((tq, 1), 0)
    sm = sm_ref[...]
    wcols = [sm[:, IDX_DIM + h:IDX_DIM + h + 1] for h in range(IDX_HEADS)]

    def score_tile(kt, carry):
        ks = pl.multiple_of(kt * ts, ts)
        ka = kia_ref[pl.ds(ks, ts), :]
        kb = kib_ref[pl.ds(ks, ts), :]
        acc = jnp.zeros((tq, ts), F32)
        for p in range(IDX_HEADS // 2):
            qp = qi_ref[:, p * LANES:(p + 1) * LANES]
            acc = acc + wcols[2 * p] * jnp.maximum(_dot_nt(qp, ka), 0.0)
            acc = acc + wcols[2 * p + 1] * jnp.maximum(_dot_nt(qp, kb), 0.0)
        kpos = ks + _iota((1, ts), 1)
        sc_ref[kt] = jnp.where(kpos <= qpos, acc * (IDX_HEADS ** -0.5), NEG)
        return carry

    lax.fori_loop(0, nk, score_tile, 0)

    def reduce_tiles(fn, comb, init):
        def body(kt, c):
            x = fn(sc_ref[kt])
            for j in range(ts // LANES):
                c = comb(c, x[:, j * LANES:(j + 1) * LANES])
            return c
        return lax.fori_loop(0, nk, body, jnp.full((tq, LANES), init, F32))

    def count_ge(t):
        part = reduce_tiles(lambda s: jnp.where(s >= t, 1.0, 0.0), lambda a, b: a + b, 0.0)
        return jnp.sum(part, axis=1, keepdims=True)

    hi = jnp.max(reduce_tiles(lambda s: s, jnp.maximum, NEG), axis=1, keepdims=True)
    lo = jnp.min(reduce_tiles(lambda s: jnp.where(s > 0.5 * NEG, s, -NEG), jnp.minimum, -NEG),
                 axis=1, keepdims=True)
    thr = _bisect_threshold(count_ge, lo, hi, float(topk))
    thr = jnp.where(qpos + 1 <= topk, NEG, thr)

    def to_bias(kt, carry):
        s = sc_ref[kt]
        keep = jnp.logical_and(s >= thr, s > 0.5 * NEG)
        sc_ref[kt] = jnp.where(keep, 0.0, NEG)
        return carry

    lax.fori_loop(0, nk, to_bias, 0)

    scale2 = (A_DH ** -0.5) * 1.4426950408889634
    rep = A_HEADS // A_KV
    for g in range(A_KV):
        qg = jnp.concatenate(
            [q_ref[:, (g * rep + r) * A_DH:(g * rep + r + 1) * A_DH] for r in range(rep)], axis=0)

        def scores(kt, g=g, qg=qg):
            ks = pl.multiple_of(kt * ts, ts)
            kk = k_ref[pl.ds(ks, ts), g * A_DH:(g + 1) * A_DH]
            return _dot_nt(qg, kk) * scale2 + jnp.concatenate([sc_ref[kt]] * rep, axis=0)

        def lane_fold(x, part, comb):
            for j in range(ts // LANES):
                part = comb(part, x[:, j * LANES:(j + 1) * LANES])
            return part

        def pass_max(kt, mpart):
            return lane_fold(scores(kt), mpart, jnp.maximum)

        mpart = lax.fori_loop(0, nk, pass_max, jnp.full((rep * tq, LANES), NEG, F32))
        m = jnp.max(mpart, axis=1, keepdims=True)

        def pass_acc(kt, c, g=g, m=m):
            lpart, acc = c
            ks = pl.multiple_of(kt * ts, ts)
            vv = v_ref[pl.ds(ks, ts), g * A_DH:(g + 1) * A_DH]
            p = jnp.exp2(scores(kt) - m)
            return lane_fold(p, lpart, lambda a, b: a + b), acc + _dot(p.astype(BF16), vv)

        lpart, acc = lax.fori_loop(0, nk, pass_acc, (jnp.zeros((rep * tq, LANES), F32),
                                                     jnp.zeros((rep * tq, A_DH), F32)))
        o = acc / jnp.sum(lpart, axis=1, keepdims=True)
        for r in range(rep):
            o_ref[:, (g * rep + r) * A_DH:(g * rep + r + 1) * A_DH] = (
                o[r * tq:(r + 1) * tq].astype(o_ref.dtype))


def _dsa_prompt(q, k, v, qi, kia, kib, small, *, batch, seq, tq=256, ts=512):
    topk = min(TOPK, seq // 4)
    nq = seq // tq

    def rows(width):
        return pl.BlockSpec((tq, width), lambda b, i: (b * nq + i, 0))

    def whole(width):
        return pl.BlockSpec((seq, width), lambda b, i: (b, 0))

    return pl.pallas_call(
        functools.partial(_dsa_prompt_kernel, tq=tq, ts=ts, topk=topk),
        out_shape=jax.ShapeDtypeStruct((batch * seq, A_HEADS * A_DH), BF16),
        grid=(batch, nq),
        in_specs=[rows(A_HEADS * A_DH), whole(A_KV * A_DH), whole(A_KV * A_DH),
                  rows(IDX_HEADS * IDX_DIM), whole(LANES), whole(LANES), rows(LANES)],
        out_specs=rows(A_HEADS * A_DH),
        scratch_shapes=[pltpu.VMEM((seq // ts, tq, ts), F32)],
        compiler_params=_cparams(("arbitrary", "arbitrary")),
        name="dsa_prompt",
    )(q, k, v, qi, kia, kib, small)


def _dsa_score_kernel(pt_ref, qi_ref, wi_ref, kinew_ref, pool_ref, sc_ref, snew_ref,
                      kbuf_ref, sem_ref, *, layer, n_pages):
    b = pl.program_id(0)
    nb = pl.num_programs(0)
    past = n_pages * PAGE

    def page_copy(bb, p, slot):
        return pltpu.make_async_copy(pool_ref.at[layer, pt_ref[bb, p]],
                                     kbuf_ref.at[slot, :, pl.ds(p * PAGE, PAGE)], sem_ref.at[slot])

    def start_all(bb, slot):
        def body(p, c):
            page_copy(bb, p, slot).start()
            return c
        lax.fori_loop(0, n_pages, body, 0, unroll=8)

    def wait_all(bb, slot):
        def body(p, c):
            page_copy(bb, p, slot).wait()
            return c
        lax.fori_loop(0, n_pages, body, 0, unroll=8)

    slot = b % 2

    @pl.when(b == 0)
    def _():
        start_all(b, slot)

    @pl.when(b + 1 < nb)
    def _():
        start_all(b + 1, 1 - slot)

    wait_all(b, slot)

    qi = qi_ref[0]
    w = wi_ref[0]
    chunk_pages = min(16, n_pages)
    for c in range(n_pages // chunk_pages):
        lanes = slice(c * chunk_pages * PAGE, (c + 1) * chunk_pages * PAGE)
        kt = kbuf_ref[slot, :, lanes].astype(BF16)
        lg = jnp.maximum(_dot(qi, kt), 0.0) * w
        row = jnp.sum(lg, axis=0, keepdims=True) * (IDX_HEADS ** -0.5)
        for p in range(chunk_pages):
            sc_ref[0, c * chunk_pages + p:c * chunk_pages + p + 1, :] = row[:, p * PAGE:(p + 1) * PAGE]

    kin = kinew_ref[0].astype(BF16).astype(F32)
    lnew = jnp.sum(qi.astype(F32) * kin, axis=1, keepdims=True)
    snew = jnp.sum(jnp.maximum(lnew, 0.0) * w, axis=0, keepdims=True) * (IDX_HEADS ** -0.5)
    snew_ref[0] = jnp.broadcast_to(snew, (1, LANES))


def _dsa_pick_kernel(dense_ref, snew_ref, sc_ref, ptv_ref, idx_ref, thr_ref, *, n_pages, topk):
    b = pl.program_id(0)
    past = n_pages * PAGE

    @pl.when(b == 0)
    def _():
        bd = dense_ref.shape[0]
        s_new = snew_ref[:, 0:1]

        def reduce_all(fn, comb, init):
            c = jnp.full((bd, LANES), init, F32)
            for j in range(past // LANES):
                c = comb(c, fn(dense_ref[:, j * LANES:(j + 1) * LANES]))
            return c

        def count_ge(t):
            part = reduce_all(lambda s: jnp.where(s >= t, 1.0, 0.0), lambda a, b_: a + b_, 0.0)
            return jnp.sum(part, axis=1, keepdims=True) + jnp.where(s_new >= t, 1.0, 0.0)

        hi = jnp.maximum(jnp.max(reduce_all(lambda s: s, jnp.maximum, NEG), axis=1, keepdims=True), s_new)
        lo = jnp.minimum(jnp.min(reduce_all(lambda s: s, jnp.minimum, -NEG), axis=1, keepdims=True), s_new)
        thr_all = _bisect_threshold(count_ge, lo, hi, float(topk))
        thr_ref[...] = jnp.broadcast_to(thr_all, thr_ref.shape)

    thr = thr_ref[pl.ds(b, 1), 0:1]
    snew = snew_ref[pl.ds(b, 1), 0:1]
    sc = sc_ref[0]

    def total(x):
        return jnp.sum(jnp.sum(x, axis=1, keepdims=True), axis=0, keepdims=True)

    gt = sc > thr
    eq = sc == thr
    n_gt = total(jnp.where(gt, 1.0, 0.0)) + jnp.where(snew > thr, 1.0, 0.0)
    need = topk - n_gt
    u_incl = (_iota((PAGE, PAGE), 0) <= _iota((PAGE, PAGE), 1)).astype(BF16)
    pg = (n_pages, n_pages)
    l_strict = (_iota(pg, 1) < _iota(pg, 0)).astype(BF16)
    u_pages = (_iota(pg, 0) <= _iota(pg, 1)).astype(BF16)
    eqf = eq.astype(BF16)
    eq_rank = _dot(eqf, u_incl) + jnp.sum(_dot(l_strict, eqf), axis=1, keepdims=True)
    sel = jnp.logical_or(gt, jnp.logical_and(eq, eq_rank <= need))
    self_ = sel.astype(BF16)
    inrow = _dot(self_, u_incl)
    ones8 = jnp.ones((SUBLANES, PAGE), BF16)
    crow = _dot_nt(ones8, self_)
    cend = _dot(crow.astype(BF16), u_pages)
    cstart = cend - crow
    slots = topk
    r = (_iota((slots, 1), 0) + 1).astype(F32)
    onehot = jnp.logical_and(cstart[0:1, :] < r, r <= cend[0:1, :])
    ohf = onehot.astype(F32)
    rowsel = _dot(onehot.astype(BF16), (sel.astype(F32) * inrow).astype(BF16))
    qr = r - jnp.sum(ohf * cstart[0:1, :], axis=1, keepdims=True)
    lane = _iota((slots, PAGE), 1)
    off = jnp.sum(jnp.where(rowsel == qr, lane, 0), axis=1, keepdims=True)
    page = jnp.sum(jnp.where(onehot, _iota((slots, n_pages), 1), 0), axis=1, keepdims=True)
    has = jnp.sum(ohf, axis=1, keepdims=True) > 0.5
    phys = jnp.sum(ohf * ptv_ref[0].astype(F32), axis=1, keepdims=True).astype(I32)
    idx_ref[0] = jnp.where(has, phys * PAGE + off, -1)


def _dsa_select(page_table, qi, wi, ki_new, pool_ki_t, *, layer):
    bd, n_pages = page_table.shape
    past = n_pages * PAGE
    topk = min(TOPK, (past + 1) // 4)
    grid_spec = pltpu.PrefetchScalarGridSpec(
        num_scalar_prefetch=1,
        grid=(bd,),
        in_specs=[pl.BlockSpec((1, IDX_HEADS, IDX_DIM), lambda b, pt: (b, 0, 0)),
                  pl.BlockSpec((1, IDX_HEADS, 1), lambda b, pt: (b, 0, 0)),
                  pl.BlockSpec((1, 1, IDX_DIM), lambda b, pt: (b, 0, 0)),
                  pl.BlockSpec(memory_space=pl.ANY)],
        out_specs=[pl.BlockSpec((1, n_pages, PAGE), lambda b, pt: (b, 0, 0)),
                   pl.BlockSpec((1, 1, LANES), lambda b, pt: (b, 0, 0))],
        scratch_shapes=[pltpu.VMEM((2, IDX_DIM, past), F32),
                        pltpu.SemaphoreType.DMA((2,))],
    )
    scores, snew = pl.pallas_call(
        functools.partial(_dsa_score_kernel, layer=layer, n_pages=n_pages),
        out_shape=[jax.ShapeDtypeStruct((bd, n_pages, PAGE), F32),
                   jax.ShapeDtypeStruct((bd, 1, LANES), F32)],
        grid_spec=grid_spec,
        compiler_params=pltpu.CompilerParams(dimension_semantics=("arbitrary",),
                                             vmem_limit_bytes=VMEM_LIMIT, disable_bounds_checks=True),
        name="dsa_score",
    )(page_table, qi, wi, ki_new, pool_ki_t)
    return pl.pallas_call(
        functools.partial(_dsa_pick_kernel, n_pages=n_pages, topk=topk),
        out_shape=jax.ShapeDtypeStruct((bd, topk, 1), I32),
        grid=(bd,),
        in_specs=[pl.BlockSpec((bd, past), lambda b: (0, 0)),
                  pl.BlockSpec((bd, LANES), lambda b: (0, 0)),
                  pl.BlockSpec((1, n_pages, PAGE), lambda b: (b, 0, 0)),
                  pl.BlockSpec((1, 1, n_pages), lambda b: (b, 0, 0))],
        out_specs=pl.BlockSpec((1, topk, 1), lambda b: (b, 0, 0)),
        scratch_shapes=[pltpu.VMEM((bd, LANES), F32)],
        compiler_params=_cparams(("arbitrary",)),
        name="dsa_pick",
    )(scores.reshape(bd, past), snew.reshape(bd, LANES), scores, page_table.reshape(bd, 1, n_pages))


def _dsa_gather_attend_kernel(idx_s, q_ref, idxv_ref, knew_ref, vnew_ref, kpool_ref, vpool_ref,
                              o_ref, kbuf_ref, vbuf_ref, sem_ref, *, layer, topk):
    b = pl.program_id(0)
    nb = pl.num_programs(0)

    def copies(bb, r, slot):
        row = jnp.maximum(idx_s[bb, r], 0)
        return (pltpu.make_async_copy(kpool_ref.at[layer, row], kbuf_ref.at[slot, r],
                                      sem_ref.at[0, slot]),
                pltpu.make_async_copy(vpool_ref.at[layer, row], vbuf_ref.at[slot, r],
                                      sem_ref.at[1, slot]))

    def start_all(bb, slot):
        def body(r, c):
            ck, cv = copies(bb, r, slot)
            ck.start()
            cv.start()
            return c
        lax.fori_loop(0, topk, body, 0, unroll=8)

    def wait_all(bb, slot):
        def body(r, c):
            ck, cv = copies(bb, r, slot)
            ck.wait()
            cv.wait()
            return c
        lax.fori_loop(0, topk, body, 0, unroll=8)

    slot = b % 2

    @pl.when(b == 0)
    def _():
        start_all(b, slot)

    @pl.when(b + 1 < nb)
    def _():
        start_all(b + 1, 1 - slot)

    wait_all(b, slot)

    q = q_ref[0]
    in_past = idxv_ref[0] >= 0
    rep = A_HEADS // A_KV
    head_group = _iota((A_HEADS, A_DH), 0) // rep
    out = jnp.zeros((A_HEADS, A_DH), F32)
    for g in range(A_KV):
        kg = jnp.where(in_past, kbuf_ref[slot, :, g, :], knew_ref[0, g:g + 1, :]).astype(BF16)
        vg = jnp.where(in_past, vbuf_ref[slot, :, g, :], vnew_ref[0, g:g + 1, :]).astype(BF16)
        s = _dot_nt(q, kg) * (A_DH ** -0.5)
        m = jnp.max(s, axis=1, keepdims=True)
        p = jnp.exp(s - m)
        p = p / jnp.sum(p, axis=1, keepdims=True)
        o = _dot(p.astype(BF16), vg)
        out = jnp.where(head_group == g, o, out)
    o_ref[0] = out.astype(o_ref.dtype)


def _dsa_gather_attend(idx, q, k_new, v_new, pool_k, pool_v, *, layer):
    bd, topk = idx.shape[:2]
    grid_spec = pltpu.PrefetchScalarGridSpec(
        num_scalar_prefetch=1,
        grid=(bd,),
        in_specs=[pl.BlockSpec((1, A_HEADS, A_DH), lambda b, i: (b, 0, 0)),
                  pl.BlockSpec((1, topk, 1), lambda b, i: (b, 0, 0)),
                  pl.BlockSpec((1, A_KV, A_DH), lambda b, i: (b, 0, 0)),
                  pl.BlockSpec((1, A_KV, A_DH), lambda b, i: (b, 0, 0)),
                  pl.BlockSpec(memory_space=pl.ANY),
                  pl.BlockSpec(memory_space=pl.ANY)],
        out_specs=pl.BlockSpec((1, A_HEADS, A_DH), lambda b, i: (b, 0, 0)),
        scratch_shapes=[pltpu.VMEM((2, topk, A_KV, A_DH), F32),
                        pltpu.VMEM((2, topk, A_KV, A_DH), F32),
                        pltpu.SemaphoreType.DMA((2, 2))],
    )
    return pl.pallas_call(
        functools.partial(_dsa_gather_attend_kernel, layer=layer, topk=topk),
        out_shape=jax.ShapeDtypeStruct((bd, A_HEADS, A_DH), BF16),
        grid_spec=grid_spec,
        compiler_params=pltpu.CompilerParams(dimension_semantics=("arbitrary",),
                                             vmem_limit_bytes=VMEM_LIMIT, disable_bounds_checks=True),
        name="dsa_gather_attend",
    )(idx.reshape(bd, topk), q, idx, k_new, v_new,
      pool_k.reshape(pool_k.shape[0], -1, A_KV, A_DH), pool_v.reshape(pool_v.shape[0], -1, A_KV, A_DH))


def _head_rms_gate(o, gain, gate):
    ms = jnp.mean(o * o, axis=-1, keepdims=True)
    return o * lax.rsqrt(ms + EPS) * gain * _silu(gate)


def _gla_prompt_kernel(qkr_ref, v_ref, sm_ref, w2_ref, b2_ref, gn_ref, o_ref, sfin_ref, s_ref,
                       *, tt, chunk):
    i = pl.program_id(1)
    ni = pl.num_programs(1)

    @pl.when(i == 0)
    def _():
        s_ref[...] = jnp.zeros_like(s_ref)

    hdk = B_HEADS * B_DK
    r_i = _iota((chunk, chunk), 0)
    c_i = _iota((chunk, chunk), 1)
    causal = c_i <= r_i
    tril = causal.astype(BF16)
    x = _dot(sm_ref[...].astype(BF16), w2_ref[...]) + b2_ref[...]
    log_a = _log_sigmoid(x) * (1.0 / B_TAU)
    for c in range(tt // chunk):
        rows = slice(c * chunk, (c + 1) * chunk)
        for h in range(B_HEADS):
            dk = slice(h * B_DK, (h + 1) * B_DK)
            dv = slice(h * B_DV, (h + 1) * B_DV)
            q = qkr_ref[rows, dk] * (B_DK ** -0.5)
            k = qkr_ref[rows, hdk + h * B_DK:hdk + (h + 1) * B_DK]
            gate = qkr_ref[rows, 2 * hdk + h * B_DV:2 * hdk + (h + 1) * B_DV]
            v = v_ref[rows, dv]
            bc = _dot_exact_lhs(tril, log_a[rows, dk])
            bl = bc[chunk - 1:chunk, :]
            qd = (q * jnp.exp(bc)).astype(BF16)
            kd = (k * jnp.exp(-bc)).astype(BF16)
            att = jnp.where(causal, _dot_nt(qd, kd), 0.0)
            s_old = s_ref[h]
            o = _dot(qd, s_old.astype(BF16)) + _dot(att.astype(BF16), v)
            kdec = k * jnp.exp(bl - bc)
            decay = jnp.broadcast_to(jnp.exp(bl), (chunk, B_DK)).T
            decay = jnp.concatenate([decay] * (B_DV // B_DK), axis=1)
            s_ref[h] = decay * s_old + _dot(kdec.T.astype(BF16), v)
            o_ref[rows, dv] = _head_rms_gate(o, gn_ref[...], gate).astype(o_ref.dtype)

    @pl.when(i == ni - 1)
    def _():
        sfin_ref[0] = s_ref[...]


def _gla_prompt(qkr, v, small, w2pad, b2, gn, *, batch, seq, tt=256, chunk=128):
    nt = seq // tt
    hdk, hdv = B_HEADS * B_DK, B_HEADS * B_DV

    def rows(width):
        return pl.BlockSpec((tt, width), lambda b, i: (b * nt + i, 0))

    def const(shape):
        return pl.BlockSpec(shape, lambda b, i: (0,) * len(shape))

    return pl.pallas_call(
        functools.partial(_gla_prompt_kernel, tt=tt, chunk=chunk),
        out_shape=[jax.ShapeDtypeStruct((batch * seq, hdv), BF16),
                   jax.ShapeDtypeStruct((batch, B_HEADS, B_DK, B_DV), F32)],
        grid=(batch, nt),
        in_specs=[rows(2 * hdk + hdv), rows(hdv), rows(LANES),
                  const((LANES, hdk)), const((1, hdk)), const((1, B_DV))],
        out_specs=[rows(hdv),
                   pl.BlockSpec((1, B_HEADS, B_DK, B_DV), lambda b, i: (b, 0, 0, 0))],
        scratch_shapes=[pltpu.VMEM((B_HEADS, B_DK, B_DV), F32)],
        compiler_params=_cparams(("arbitrary", "arbitrary")),
        name="gla_prompt",
    )(qkr, v, small, w2pad, b2.reshape(1, hdk), gn.reshape(1, B_DV))


def _row_to_col(row):
    n = row.shape[1]
    eye = _iota((n, n), 0) == _iota((n, n), 1)
    return jnp.sum(jnp.where(eye, jnp.broadcast_to(row, (n, n)), 0.0), axis=1, keepdims=True)


def _gla_step_kernel(qkr_ref, v_ref, sm_ref, w2_ref, b2_ref, gn_ref, s_ref, o_ref, snew_ref):
    hdk = B_HEADS * B_DK
    smb = jnp.broadcast_to(sm_ref[0], (SUBLANES, LANES)).astype(BF16)
    x = _dot(smb, w2_ref[...])[0:1, :] + b2_ref[...]
    log_a = _log_sigmoid(x) * (1.0 / B_TAU)
    for h in range(B_HEADS):
        dk = slice(h * B_DK, (h + 1) * B_DK)
        dv = slice(h * B_DV, (h + 1) * B_DV)
        g_row = log_a[:, dk]
        q_row = qkr_ref[0, :, dk] * (B_DK ** -0.5)
        k_row = qkr_ref[0, :, hdk + h * B_DK:hdk + (h + 1) * B_DK]
        gate = qkr_ref[0, :, 2 * hdk + h * B_DV:2 * hdk + (h + 1) * B_DV]
        v_row = v_ref[0, :, dv].astype(F32)
        qd_row = q_row * jnp.exp(g_row)
        kd_row = k_row * jnp.exp(-g_row)
        s_old = s_ref[0, h]
        snew_ref[0, h] = _row_to_col(jnp.exp(g_row)) * s_old + _row_to_col(k_row) * v_row
        att = jnp.sum(qd_row * kd_row, axis=1, keepdims=True)
        o = jnp.sum(_row_to_col(qd_row) * s_old, axis=0, keepdims=True) + att * v_row
        o_ref[0, :, dv] = _head_rms_gate(o, gn_ref[...], gate).astype(o_ref.dtype)


def _gla_step(qkr, v, small, w2pad, b2, gn, state):
    bd = qkr.shape[0]
    hdk, hdv = B_HEADS * B_DK, B_HEADS * B_DV

    def row(width):
        return pl.BlockSpec((1, 1, width), lambda b: (b, 0, 0))

    def const(shape):
        return pl.BlockSpec(shape, lambda b: (0,) * len(shape))

    st = pl.BlockSpec((1, B_HEADS, B_DK, B_DV), lambda b: (b, 0, 0, 0))
    return pl.pallas_call(
        _gla_step_kernel,
        out_shape=[jax.ShapeDtypeStruct((bd, 1, hdv), BF16),
                   jax.ShapeDtypeStruct((bd, B_HEADS, B_DK, B_DV), F32)],
        grid=(bd,),
        in_specs=[row(2 * hdk + hdv), row(hdv), row(LANES),
                  const((LANES, hdk)), const((1, hdk)), const((1, B_DV)), st],
        out_specs=[row(hdv), st],
        compiler_params=_cparams(("arbitrary",)),
        name="gla_step",
    )(qkr.reshape(bd, 1, -1), v.reshape(bd, 1, -1), small.reshape(bd, 1, -1),
      w2pad, b2.reshape(1, hdk), gn.reshape(1, B_DV), state)


def _head_rms_sig(h, gain, gate):
    ms = jnp.mean(h * h, axis=-1, keepdims=True)
    return h * lax.rsqrt(ms + EPS) * gain * _sigmoid(gate)


def _mlstm_prompt_kernel(qko_ref, v_ref, sm_ref, bias_ref, hn_ref, o_ref, cfin_ref, nfin_ref, mfin_ref,
                         c_ref, n_ref, m_ref, *, tt, chunk):
    i = pl.program_id(1)
    ni = pl.num_programs(1)

    @pl.when(i == 0)
    def _():
        c_ref[...] = jnp.zeros_like(c_ref)
        n_ref[...] = jnp.zeros_like(n_ref)
        m_ref[...] = jnp.zeros_like(m_ref)

    hdk = C_HEADS * C_DK
    r_i = _iota((chunk, chunk), 0)
    c_i = _iota((chunk, chunk), 1)
    causal = c_i <= r_i
    tril = causal.astype(BF16)
    pre = sm_ref[...] + bias_ref[...]
    for c in range(tt // chunk):
        rows = slice(c * chunk, (c + 1) * chunk)
        for h in range(C_HEADS):
            dk = slice(h * C_DK, (h + 1) * C_DK)
            dv = slice(h * C_DV, (h + 1) * C_DV)
            icol = pre[rows, h:h + 1]
            fcol = _log_sigmoid(pre[rows, C_HEADS + h:C_HEADS + h + 1])
            bmat = _dot_exact_lhs(tril, jnp.broadcast_to(fcol, (chunk, chunk)))
            bcol = bmat[:, 0:1]
            b_t = bmat.T
            i_t = jnp.broadcast_to(icol, (chunk, chunk)).T
            m_prev = m_ref[h, 0:1, 0:1]
            log_d = jnp.where(causal, bmat - b_t + i_t, NEG)
            inter = bcol + m_prev
            m_t = jnp.maximum(inter, jnp.max(log_d, axis=1, keepdims=True))
            dmat = jnp.exp(log_d - m_t)
            g = jnp.exp(inter - m_t)
            q = qko_ref[rows, dk]
            ks = qko_ref[rows, hdk + h * C_DK:hdk + (h + 1) * C_DK] * (C_DK ** -0.5)
            gate = qko_ref[rows, 2 * hdk + h * C_DV:2 * hdk + (h + 1) * C_DV]
            v = v_ref[rows, dv]
            qb = q.astype(BF16)
            qk = _dot_nt(qb, ks.astype(BF16)) * dmat
            c_old = c_ref[h]
            n_old = n_ref[h, 0:1, :]
            num = g * _dot(qb, c_old.astype(BF16)) + _dot(qk.astype(BF16), v)
            den = g * jnp.sum(q * n_old, axis=1, keepdims=True) + jnp.sum(qk, axis=1, keepdims=True)
            hh = num / jnp.maximum(jnp.abs(den), jnp.exp(-m_t))
            o_ref[rows, dv] = _head_rms_sig(hh, hn_ref[...], gate).astype(o_ref.dtype)
            m_last = m_t[chunk - 1:chunk, :]
            g_last = g[chunk - 1:chunk, :]
            wcol = jnp.exp(bcol[chunk - 1:chunk, :] - bcol + icol - m_last)
            kw = ks * wcol
            c_ref[h] = g_last * c_old + _dot(kw.T.astype(BF16), v)
            n_ref[h] = jnp.broadcast_to(g_last * n_old + jnp.sum(kw, axis=0, keepdims=True),
                                        (SUBLANES, C_DK))
            m_ref[h] = jnp.broadcast_to(m_last, (SUBLANES, LANES))

    @pl.when(i == ni - 1)
    def _():
        cfin_ref[0] = c_ref[...]
        nfin_ref[0] = n_ref[...]
        mfin_ref[0] = m_ref[...]


def _mlstm_prompt(qko, v, small, bias, hn, *, batch, seq, tt=256, chunk=128):
    nt = seq // tt
    hdk, hdv = C_HEADS * C_DK, C_HEADS * C_DV

    def rows(width):
        return pl.BlockSpec((tt, width), lambda b, i: (b * nt + i, 0))

    def const(shape):
        return pl.BlockSpec(shape, lambda b, i: (0,) * len(shape))

    def per_batch(shape):
        return pl.BlockSpec((1,) + shape, lambda b, i: (b,) + (0,) * len(shape))

    return pl.pallas_call(
        functools.partial(_mlstm_prompt_kernel, tt=tt, chunk=chunk),
        out_shape=[jax.ShapeDtypeStruct((batch * seq, hdv), BF16),
                   jax.ShapeDtypeStruct((batch, C_HEADS, C_DK, C_DV), F32),
                   jax.ShapeDtypeStruct((batch, C_HEADS, SUBLANES, C_DK), F32),
                   jax.ShapeDtypeStruct((batch, C_HEADS, SUBLANES, LANES), F32)],
        grid=(batch, nt),
        in_specs=[rows(2 * hdk + hdv), rows(hdv), rows(LANES), const((1, LANES)), const((1, C_DV))],
        out_specs=[rows(hdv), per_batch((C_HEADS, C_DK, C_DV)),
                   per_batch((C_HEADS, SUBLANES, C_DK)), per_batch((C_HEADS, SUBLANES, LANES))],
        scratch_shapes=[pltpu.VMEM((C_HEADS, C_DK, C_DV), F32),
                        pltpu.VMEM((C_HEADS, SUBLANES, C_DK), F32),
                        pltpu.VMEM((C_HEADS, SUBLANES, LANES), F32)],
        compiler_params=_cparams(("arbitrary", "arbitrary")),
        name="mlstm_prompt",
    )(qko, v, small, bias, hn.reshape(1, C_DV))


def _mlstm_step_kernel(qko_ref, v_ref, sm_ref, bias_ref, hn_ref, c_ref, n_ref, m_ref,
                       o_ref, cnew_ref, nnew_ref, mnew_ref):
    hdk = C_HEADS * C_DK
    pre = sm_ref[0] + bias_ref[...]
    m_all = m_ref[0]
    m_out = jnp.zeros_like(m_all)
    for h in range(C_HEADS):
        dk = slice(h * C_DK, (h + 1) * C_DK)
        dv = slice(h * C_DV, (h + 1) * C_DV)
        i_pre = pre[:, h:h + 1]
        log_f = _log_sigmoid(pre[:, C_HEADS + h:C_HEADS + h + 1])
        m_prev = m_all[:, h:h + 1]
        inter = log_f + m_prev
        m_t = jnp.maximum(inter, i_pre)
        d = jnp.exp(i_pre - m_t)
        g = jnp.exp(inter - m_t)
        q_row = qko_ref[0, :, dk]
        ks_row = qko_ref[0, :, hdk + h * C_DK:hdk + (h + 1) * C_DK] * (C_DK ** -0.5)
        gate = qko_ref[0, :, 2 * hdk + h * C_DV:2 * hdk + (h + 1) * C_DV]
        v_row = v_ref[0, :, dv].astype(F32)
        c_old = c_ref[0, h]
        n_old = n_ref[0, h:h + 1, :]
        qk = jnp.sum(q_row * ks_row, axis=1, keepdims=True) * d
        num = g * jnp.sum(_row_to_col(q_row) * c_old, axis=0, keepdims=True) + qk * v_row
        den = g * jnp.sum(q_row * n_old, axis=1, keepdims=True) + qk
        hh = num / jnp.maximum(jnp.abs(den), jnp.exp(-m_t))
        o_ref[0, :, dv] = _head_rms_sig(hh, hn_ref[...], gate).astype(o_ref.dtype)
        cnew_ref[0, h] = g * c_old + _row_to_col(d * ks_row) * v_row
        nnew_ref[0, h:h + 1, :] = g * n_old + d * ks_row
        m_out = jnp.where(_iota(m_all.shape, 1) == h, m_t, m_out)
    mnew_ref[0] = m_out


def _mlstm_step(qko, v, small, bias, hn, c0, n0, m0):
    bd = qko.shape[0]
    hdk, hdv = C_HEADS * C_DK, C_HEADS * C_DV

    def row(width):
        return pl.BlockSpec((1, 1, width), lambda b: (b, 0, 0))

    def const(shape):
        return pl.BlockSpec(shape, lambda b: (0,) * len(shape))

    cs = pl.BlockSpec((1, C_HEADS, C_DK, C_DV), lambda b: (b, 0, 0, 0))
    ns = pl.BlockSpec((1, C_HEADS, C_DK), lambda b: (b, 0, 0))
    return pl.pallas_call(
        _mlstm_step_kernel,
        out_shape=[jax.ShapeDtypeStruct((bd, 1, hdv), BF16),
                   jax.ShapeDtypeStruct((bd, C_HEADS, C_DK, C_DV), F32),
                   jax.ShapeDtypeStruct((bd, C_HEADS, C_DK), F32),
                   jax.ShapeDtypeStruct((bd, 1, C_HEADS), F32)],
        grid=(bd,),
        in_specs=[row(2 * hdk + hdv), row(hdv), row(LANES), const((1, LANES)), const((1, C_DV)),
                  cs, ns, row(C_HEADS)],
        out_specs=[row(hdv), cs, ns, row(C_HEADS)],
        compiler_params=_cparams(("arbitrary",)),
        name="mlstm_step",
    )(qko.reshape(bd, 1, -1), v.reshape(bd, 1, -1), small.reshape(bd, 1, -1), bias,
      hn.reshape(1, C_DV), c0, n0, m0.reshape(bd, 1, C_HEADS))


def _swa_prompt_kernel(sink_ref, q_ref, kvp_ref, kvc_ref, o_ref, *, w):
    j = pl.program_id(1)
    band = jnp.concatenate([kvp_ref[...], kvc_ref[...]], axis=0)
    kk = band[:, :LANES]
    vv = band[:, LANES:]
    lane = _iota((2 * w, LANES), 1)
    qi = _iota((w, 2 * w), 0)
    kj = _iota((w, 2 * w), 1)
    allowed = jnp.logical_and(kj >= qi, kj <= qi + w)
    allowed = jnp.logical_and(allowed, jnp.logical_or(j > 0, kj >= w))
    rep = D_HEADS // D_KV
    pairs = rep // 2
    allowed = jnp.concatenate([allowed] * pairs, axis=0)
    zero = jnp.zeros_like(kk)
    for g in range(D_KV):
        own = (lane // D_DH) == g
        k_own = jnp.where(own, kk, zero)
        v_own = jnp.where(own, vv, zero)
        k_oth = pltpu.roll(k_own, D_DH, 1)
        v_oth = pltpu.roll(v_own, D_DH, 1)
        k_lo, k_hi = (k_own, k_oth) if g == 0 else (k_oth, k_own)
        v_lo, v_hi = (v_own, v_oth) if g == 0 else (v_oth, v_own)
        qs = jnp.concatenate([q_ref[:, (g * pairs + p) * LANES:(g * pairs + p + 1) * LANES]
                              for p in range(pairs)], axis=0)
        outs = []
        for half, (kh, vh) in enumerate(((k_lo, v_lo), (k_hi, v_hi))):
            kh = kh.astype(BF16)
            vh = vh.astype(BF16)
            s = _dot_nt(qs, kh) * (D_DH ** -0.5)
            s = jnp.where(allowed, s, NEG)
            sink = jnp.concatenate(
                [jnp.full((w, 1), sink_ref[2 * (g * pairs + p) + half], F32) for p in range(pairs)],
                axis=0)
            m = jnp.maximum(jnp.max(s, axis=1, keepdims=True), sink)
            e = jnp.where(allowed, jnp.exp(s - m), 0.0)
            p_ = e / (jnp.sum(e, axis=1, keepdims=True) + jnp.exp(sink - m))
            outs.append(_dot(p_.astype(BF16), vh))
        o = outs[0] + outs[1]
        for p in range(pairs):
            o_ref[:, (g * pairs + p) * LANES:(g * pairs + p + 1) * LANES] = (
                o[p * w:(p + 1) * w].astype(o_ref.dtype))


def _swa_prompt(q, kv, sinks, *, batch, seq):
    w = WINDOW
    nb = seq // w
    grid_spec = pltpu.PrefetchScalarGridSpec(
        num_scalar_prefetch=0,
        grid=(batch, nb),
        in_specs=[pl.BlockSpec(memory_space=pltpu.SMEM),
                  pl.BlockSpec((w, D_HEADS * D_DH), lambda b, j: (b * nb + j, 0)),
                  pl.BlockSpec((w, 2 * LANES), lambda b, j: (b * nb + jnp.maximum(j - 1, 0), 0)),
                  pl.BlockSpec((w, 2 * LANES), lambda b, j: (b * nb + j, 0))],
        out_specs=pl.BlockSpec((w, D_HEADS * D_DH), lambda b, j: (b * nb + j, 0)),
    )
    return pl.pallas_call(
        functools.partial(_swa_prompt_kernel, w=w),
        out_shape=jax.ShapeDtypeStruct((batch * seq, D_HEADS * D_DH), BF16),
        grid_spec=grid_spec,
        compiler_params=_cparams(("arbitrary", "arbitrary")),
        name="swa_prompt",
    )(sinks, q, kv, kv)


def _swa_step_kernel(q_ref, kvn_ref, bk_ref, bv_ref, sink_ref, o_ref, nk_ref, nv_ref):
    w = bk_ref.shape[1]
    rep = D_HEADS // D_KV
    q = q_ref[0]
    dup = (_iota((D_DH, LANES), 1) % D_DH == _iota((D_DH, LANES), 0)).astype(BF16)
    q2 = _dot(q, dup)
    own = (_iota((D_HEADS, LANES), 1) // D_DH) == (_iota((D_HEADS, LANES), 0) // rep)
    qm = jnp.where(own, q2, 0.0)
    bk = bk_ref[0]
    bv = bv_ref[0]
    k_new = kvn_ref[0, :, :LANES]
    v_new = kvn_ref[0, :, LANES:]
    s = _dot_nt(qm.astype(BF16), bk.astype(BF16)) * (D_DH ** -0.5)
    s_new = jnp.sum(qm * k_new.astype(BF16).astype(F32), axis=1, keepdims=True) * (D_DH ** -0.5)
    sink = sink_ref[...]
    m = jnp.maximum(jnp.maximum(jnp.max(s, axis=1, keepdims=True), s_new), sink)
    e = jnp.exp(s - m)
    e_new = jnp.exp(s_new - m)
    den = jnp.sum(e, axis=1, keepdims=True) + e_new + jnp.exp(sink - m)
    o = _dot((e / den).astype(BF16), bv.astype(BF16))
    o = o + (e_new / den).astype(BF16).astype(F32) * v_new.astype(BF16).astype(F32)
    o = jnp.where(own, o, 0.0)
    o = o + pltpu.roll(o, D_DH, 1)
    o_ref[0] = o[:, :D_DH].astype(o_ref.dtype)
    nk_ref[0, 0:w - 1, :] = bk_ref[0, 1:w, :]
    nk_ref[0, w - 1:w, :] = k_new
    nv_ref[0, 0:w - 1, :] = bv_ref[0, 1:w, :]
    nv_ref[0, w - 1:w, :] = v_new


def _swa_step(q, kv_new, buf_k, buf_v, sinks):
    bd = q.shape[0]
    w = buf_k.shape[1]
    buf = pl.BlockSpec((1, w, LANES), lambda b: (b, 0, 0))
    return pl.pallas_call(
        _swa_step_kernel,
        out_shape=[jax.ShapeDtypeStruct((bd, D_HEADS, D_DH), BF16),
                   jax.ShapeDtypeStruct((bd, w, LANES), F32),
                   jax.ShapeDtypeStruct((bd, w, LANES), F32)],
        grid=(bd,),
        in_specs=[pl.BlockSpec((1, D_HEADS, D_DH), lambda b: (b, 0, 0)),
                  pl.BlockSpec((1, 1, 2 * LANES), lambda b: (b, 0, 0)),
                  buf, buf,
                  pl.BlockSpec((D_HEADS, 1), lambda b: (0, 0))],
        out_specs=[pl.BlockSpec((1, D_HEADS, D_DH), lambda b: (b, 0, 0)), buf, buf],
        compiler_params=_cparams(("arbitrary",)),
        name="swa_step",
    )(q.reshape(bd, D_HEADS, D_DH), kv_new.reshape(bd, 1, 2 * LANES),
      buf_k.reshape(bd, w, LANES), buf_v.reshape(bd, w, LANES), sinks.reshape(D_HEADS, 1))


def _even_weights(w, a_qn, a_kn, a_kin, b_w2):
    o = [0]
    for s in (A_HEADS * A_DH, A_KV * A_DH, A_KV * A_DH, IDX_HEADS * IDX_DIM, IDX_DIM, IDX_HEADS,
              B_HEADS * B_DK, B_HEADS * B_DK, B_HEADS * B_DV, B_RANK, B_HEADS * B_DV):
        o.append(o[-1] + s)
    d = w.shape[0]
    col = lambda i: w[:, o[i]:o[i + 1]]
    pad = LANES - (IDX_DIM + IDX_HEADS + B_RANK)
    return dict(
        vb=col(8).astype(BF16),
        small=jnp.concatenate([col(4), col(5), col(9), jnp.zeros((d, pad), w.dtype)], axis=1).astype(BF16),
        qkr=jnp.concatenate([col(6), col(7), col(10)], axis=1).astype(BF16),
        qa_gain=jnp.tile(a_qn, A_HEADS),
        ka_gain=jnp.tile(a_kn, A_KV),
        small_gain=jnp.concatenate([a_kin, jnp.zeros((LANES - IDX_DIM,), F32)]),
        w2pad=jnp.zeros((LANES, B_HEADS * B_DK), F32).at[
            IDX_DIM + IDX_HEADS:IDX_DIM + IDX_HEADS + B_RANK].set(b_w2).astype(BF16),
    )


def _odd_weights(w, c_bi, c_bf, d_qn, d_kn):
    o = [0]
    for s in (C_HEADS * C_DK, C_HEADS * C_DK, C_HEADS * C_DV, C_HEADS, C_HEADS, C_HEADS * C_DV,
              D_HEADS * D_DH, D_KV * D_DH, D_KV * D_DH):
        o.append(o[-1] + s)
    d = w.shape[0]
    col = lambda i: w[:, o[i]:o[i + 1]]
    return dict(
        qko=jnp.concatenate([col(0), col(1), col(5)], axis=1).astype(BF16),
        small=jnp.concatenate([col(3), col(4), jnp.zeros((d, LANES - 2 * C_HEADS), w.dtype)],
                              axis=1).astype(BF16),
        qd=col(6).astype(BF16),
        kvd=jnp.concatenate([col(7), col(8)], axis=1).astype(BF16),
        qd_gain=jnp.tile(d_qn, D_HEADS),
        kd_gain=jnp.concatenate([jnp.tile(d_kn, D_KV), jnp.ones((LANES,), F32)]),
        bias=jnp.concatenate([c_bi, c_bf, jnp.zeros((LANES - 2 * C_HEADS,), F32)]).reshape(1, LANES),
    )


def _even_in(h, w_all, e, ew):
    n_q, n_kv, n_qi = A_HEADS * A_DH, A_KV * A_DH, IDX_HEADS * IDX_DIM
    q, = _proj(h, w_all, "rms128", (BF16,), ew["qa_gain"], window=(e, 0, n_q))
    k32, k16 = _proj(h, w_all, "rms128", (F32, BF16), ew["ka_gain"], window=(e, n_q, n_kv))
    v32, v16 = _proj(h, w_all, "raw", (F32, BF16), window=(e, n_q + n_kv, n_kv))
    qi, = _proj(h, w_all, "raw", (BF16,), window=(e, n_q + 2 * n_kv, n_qi), scale=IDX_DIM ** -0.5)
    vb, = _proj(h, ew["vb"], "raw", (BF16,))
    small, kia, kib = _proj(h, ew["small"], "small_even", (F32, BF16, BF16), ew["small_gain"])
    qkr, = _proj(h, ew["qkr"], "raw", (F32,))
    return dict(q=q, k32=k32, k16=k16, v32=v32, v16=v16, qi=qi, vb=vb,
                small=small, kia=kia, kib=kib, qkr=qkr)


def _odd_in(h, w_all, o, ow):
    qko, = _proj(h, ow["qko"], "raw", (F32,))
    vc, = _proj(h, w_all, "raw", (BF16,), window=(o, 2 * C_HEADS * C_DK, C_HEADS * C_DV))
    small, = _proj(h, ow["small"], "raw", (F32,))
    qd, = _proj(h, ow["qd"], "rms64", (BF16,), ow["qd_gain"])
    kv32, = _proj(h, ow["kvd"], "kd_vd", (F32,), ow["kd_gain"])
    return dict(qko=qko, vc=vc, small=small, qd=qd, kv32=kv32)


def kernel(x_prompt, x_sample, cache_a_k, cache_a_v, cache_a_kidx, state_b_s, state_c_c, state_c_n,
           state_c_m, cache_d_k, cache_d_v, state_ffn_conv, page_table, norm_mix, norm_ffn,
           even_w_in, even_w_out, a_q_norm, a_k_norm, a_kidx_norm, b_gate_w2, b_gate_b, b_head_norm,
           odd_w_in, odd_w_out, c_i_bias, c_f_bias, c_head_norm, d_q_norm, d_k_norm, d_sinks,
           ffn_w_up, ffn_conv_w, ffn_conv_b, ffn_w_down):
    bp, seq, d_model = x_prompt.shape
    bd = x_sample.shape[0]
    depth = norm_mix.shape[0]
    d_ff = ffn_w_down.shape[1]
    mp = bp * seq
    xp = x_prompt.reshape(mp, d_model)
    xs = x_sample.reshape(bd, d_model)
    hp = _rms_cast(xp, norm_mix[0], tm=512)
    hs = _rms_cast(xs, norm_mix[0], tm=bd)
    pool_ki_t = jnp.swapaxes(cache_a_kidx, 2, 3)

    p_out = {k: [] for k in ("ak", "av", "aki", "bs", "cc", "cn", "cm", "dk", "dv", "conv")}
    s_out = {k: [] for k in p_out}
    for layer in range(depth):
        if layer % 2 == 0:
            e = layer // 2
            ew = _even_weights(even_w_in[e], a_q_norm[e], a_k_norm[e], a_kidx_norm[e], b_gate_w2[e])
            w_out = even_w_out[e].astype(BF16)
            z = _even_in(hp, even_w_in, e, ew)
            a_o = _dsa_prompt(z["q"], z["k16"], z["v16"], z["qi"], z["kia"], z["kib"], z["small"],
                              batch=bp, seq=seq)
            g_o, s_fin = _gla_prompt(z["qkr"], z["vb"], z["small"], ew["w2pad"], b_gate_b[e],
                                     b_head_norm[e], batch=bp, seq=seq)
            p_out["ak"].append(z["k32"].reshape(bp, seq, A_KV, A_DH))
            p_out["av"].append(z["v32"].reshape(bp, seq, A_KV, A_DH))
            p_out["aki"].append(z["small"][:, :IDX_DIM].reshape(bp, seq, IDX_DIM))
            p_out["bs"].append(s_fin)
            mix_p = (a_o, g_o)
            z = _even_in(hs, even_w_in, e, ew)
            small = z["small"]
            idx = _dsa_select(page_table,
                              z["qi"].reshape(bd, IDX_HEADS, IDX_DIM),
                              small[:, IDX_DIM:IDX_DIM + IDX_HEADS].reshape(bd, IDX_HEADS, 1),
                              small[:, :IDX_DIM].reshape(bd, 1, IDX_DIM),
                              pool_ki_t, layer=e)
            a_o = _dsa_gather_attend(idx, z["q"].reshape(bd, A_HEADS, A_DH),
                                     z["k32"].reshape(bd, A_KV, A_DH), z["v32"].reshape(bd, A_KV, A_DH),
                                     cache_a_k, cache_a_v, layer=e)
            g_o, s_new = _gla_step(z["qkr"], z["vb"], small, ew["w2pad"], b_gate_b[e],
                                   b_head_norm[e], state_b_s[e])
            s_out["ak"].append(z["k32"].reshape(bd, 1, A_KV, A_DH))
            s_out["av"].append(z["v32"].reshape(bd, 1, A_KV, A_DH))
            s_out["aki"].append(small[:, :IDX_DIM].reshape(bd, 1, IDX_DIM))
            s_out["bs"].append(s_new)
            mix_s = (a_o.reshape(bd, A_HEADS * A_DH), g_o.reshape(bd, B_HEADS * B_DV))
        else:
            o = layer // 2
            ow = _odd_weights(odd_w_in[o], c_i_bias[o], c_f_bias[o], d_q_norm[o], d_k_norm[o])
            w_out = odd_w_out[o].astype(BF16)
            z = _odd_in(hp, odd_w_in, o, ow)
            c_h, cf, nf, mf = _mlstm_prompt(z["qko"], z["vc"], z["small"], ow["bias"], c_head_norm[o],
                                            batch=bp, seq=seq)
            d_o = _swa_prompt(z["qd"], z["kv32"], d_sinks[o], batch=bp, seq=seq)
            kv = z["kv32"].reshape(bp, seq, 2, D_KV, D_DH)[:, seq - WINDOW:]
            p_out["cc"].append(cf)
            p_out["cn"].append(nf[:, :, 0, :])
            p_out["cm"].append(mf[:, :, 0, 0])
            p_out["dk"].append(kv[:, :, 0])
            p_out["dv"].append(kv[:, :, 1])
            mix_p = (c_h, d_o)
            z = _odd_in(hs, odd_w_in, o, ow)
            c_h, c_new, n_new, m_new = _mlstm_step(z["qko"], z["vc"], z["small"], ow["bias"],
                                                   c_head_norm[o], state_c_c[o], state_c_n[o],
                                                   state_c_m[o])
            d_o, nbk, nbv = _swa_step(z["qd"], z["kv32"], cache_d_k[o], cache_d_v[o], d_sinks[o])
            s_out["cc"].append(c_new)
            s_out["cn"].append(n_new)
            s_out["cm"].append(m_new.reshape(bd, C_HEADS))
            s_out["dk"].append(nbk.reshape(bd, WINDOW, D_KV, D_DH))
            s_out["dv"].append(nbv.reshape(bd, WINDOW, D_KV, D_DH))
            mix_s = (c_h.reshape(bd, C_HEADS * C_DV), d_o.reshape(bd, D_HEADS * D_DH))

        xp, hfp = _outproj(mix_p[0], mix_p[1], w_out, xp, norm_ffn[layer])
        xs, hfs = _outproj(mix_s[0], mix_s[1], w_out, xs, norm_ffn[layer])

        xp, glp = _ffn(xp, hfp, ffn_w_up, ffn_w_down, layer, ffn_conv_w[layer], ffn_conv_b[layer],
                       jnp.zeros((bp, SUBLANES, d_ff), F32), seq_len=seq, stepwise=False)
        xs, gls = _ffn(xs, hfs, ffn_w_up, ffn_w_down, layer, ffn_conv_w[layer], ffn_conv_b[layer],
                       state_ffn_conv[layer].swapaxes(0, 1), seq_len=1, stepwise=True)
        if layer + 1 < depth:
            hp = _rms_cast(xp, norm_mix[layer + 1], tm=512)
            hs = _rms_cast(xs, norm_mix[layer + 1], tm=bd)
        tiles = glp.shape[0] // bp
        p_out["conv"].append(glp.reshape(bp, tiles, SUBLANES, d_ff)[:, -1, SUBLANES - (CONV_W - 1):])
        s_out["conv"].append(jnp.stack([state_ffn_conv[layer][:, 1], gls], axis=1))

    order = ("ak", "av", "aki", "bs", "cc", "cn", "cm", "dk", "dv", "conv")
    return (xp.reshape(bp, seq, d_model), xs.reshape(bd, 1, d_model),
            *[jnp.stack(p_out[k]) for k in order],
            *[jnp.stack(s_out[k]) for k in order])
```

```python
import functools

import jax
import jax.numpy as jnp
from jax import lax
from jax.experimental import pallas as pl
from jax.experimental.pallas import tpu as pltpu

F32 = jnp.float32
BF16 = jnp.bfloat16
I32 = jnp.int32

EPS = 1e-6
NEG = -1e30

PAGE = 128
A_HEADS, A_KV, A_DH = 8, 4, 128
IDX_HEADS, IDX_DIM = 16, 64
TOPK = 256
B_HEADS, B_DK, B_DV, B_RANK, B_TAU = 4, 128, 256, 16, 16.0
C_HEADS, C_DK, C_DV = 4, 128, 256
D_HEADS, D_KV, D_DH = 16, 2, 64
WINDOW = 128
CONV_W = 3

VMEM_LIMIT = 56 * 1024 * 1024
LANES = 128
SUBLANES = 8


def _cparams(sem):
    return pltpu.CompilerParams(dimension_semantics=sem, vmem_limit_bytes=VMEM_LIMIT)


def _dot(a, b):
    return jnp.dot(a, b, preferred_element_type=F32)


def _dot_nt(a, b):
    return lax.dot_general(a, b, (((1,), (1,)), ((), ())), preferred_element_type=F32)


def _split3(x):
    hi = x.astype(BF16)
    r1 = x - hi.astype(F32)
    mid = r1.astype(BF16)
    lo = (r1 - mid.astype(F32)).astype(BF16)
    return hi, mid, lo


def _dot_exact_lhs(a_bf16, x):
    hi, mid, lo = _split3(x)
    return _dot(a_bf16, hi) + _dot(a_bf16, mid) + _dot(a_bf16, lo)


def _log_sigmoid(x):
    return jnp.minimum(x, 0.0) - jnp.log(1.0 + jnp.exp(-jnp.abs(x)))


def _sigmoid(x):
    return 0.5 + 0.5 * jnp.tanh(0.5 * x)


def _silu(x):
    return x * _sigmoid(x)


def _iota(shape, dim):
    return lax.broadcasted_iota(I32, shape, dim)


def _rms_cast_kernel(x_ref, g_ref, o_ref):
    x = x_ref[...]
    ms = jnp.mean(x * x, axis=-1, keepdims=True)
    o_ref[...] = (x * lax.rsqrt(ms + EPS) * g_ref[...]).astype(o_ref.dtype)


def _rms_cast(x, gain, tm):
    m, d = x.shape
    return pl.pallas_call(
        _rms_cast_kernel,
        out_shape=jax.ShapeDtypeStruct((m, d), BF16),
        grid=(m // tm,),
        in_specs=[pl.BlockSpec((tm, d), lambda i: (i, 0)),
                  pl.BlockSpec((1, d), lambda i: (0, 0))],
        out_specs=pl.BlockSpec((tm, d), lambda i: (i, 0)),
        compiler_params=_cparams(("arbitrary",)),
        name="rms_cast",
    )(x, gain.reshape(1, d))


def _group_rms(acc, gain, width):
    tn = acc.shape[1]
    outs = []
    for c in range(tn // LANES):
        blk = acc[:, c * LANES:(c + 1) * LANES]
        sq = blk * blk
        if width == LANES:
            ms = jnp.sum(sq, axis=-1, keepdims=True) * (1.0 / width)
        else:
            low = _iota(blk.shape, 1) < width
            s_lo = jnp.sum(jnp.where(low, sq, 0.0), axis=-1, keepdims=True)
            s_hi = jnp.sum(jnp.where(low, 0.0, sq), axis=-1, keepdims=True)
            ms = jnp.where(low, s_lo, s_hi) * (1.0 / width)
        outs.append(blk * lax.rsqrt(ms + EPS) * gain[:, c * LANES:(c + 1) * LANES])
    return outs[0] if len(outs) == 1 else jnp.concatenate(outs, axis=1)


def _proj_kernel(h_ref, w_ref, *rest, mode, n_out, scale):
    acc = _dot_nt(h_ref[...], w_ref[...].astype(BF16))
    outs = rest[-n_out:]
    if mode == "raw":
        y = acc if scale is None else acc * scale
    elif mode == "rms128":
        y = _group_rms(acc, rest[0][...], 128)
    elif mode == "rms64":
        y = _group_rms(acc, rest[0][...], 64)
    elif mode == "kd_vd":
        y = jnp.concatenate([_group_rms(acc[:, :LANES], rest[0][:, :LANES], 64), acc[:, LANES:]], axis=1)
    elif mode == "small_even":
        low = _iota(acc.shape, 1) < IDX_DIM
        ms = jnp.sum(jnp.where(low, acc * acc, 0.0), axis=-1, keepdims=True) * (1.0 / IDX_DIM)
        kin = acc * lax.rsqrt(ms + EPS) * rest[0][...]
        y = jnp.where(low, kin, acc)
        ka = jnp.where(low, kin, 0.0)
        outs[1][...] = ka.astype(BF16)
        outs[2][...] = pltpu.roll(ka, IDX_DIM, 1).astype(BF16)
        outs[0][...] = y
        return
    else:
        raise ValueError(mode)
    for o in outs:
        o[...] = y.astype(o.dtype)


def _proj(h, w, mode, out_dtypes, gain=None, tm=2048, tn=512, window=None, scale=None):
    m, k = h.shape
    if window is None:
        n = w.shape[0]
        tn = min(tn, n)
        w_spec = pl.BlockSpec((tn, k), lambda i, j: (j, 0))
    else:
        layer, row0, n = window
        tn = min(tn, n)
        assert row0 % tn == 0
        w_spec = pl.BlockSpec((None, tn, k), lambda i, j: (layer, row0 // tn + j, 0))
    tm = min(tm, m)
    in_specs = [pl.BlockSpec((tm, k), lambda i, j: (i, 0)), w_spec]
    args = [h, w]
    if gain is not None:
        in_specs.append(pl.BlockSpec((1, tn), lambda i, j: (0, j)))
        args.append(gain.reshape(1, n).astype(F32))
    outs = pl.pallas_call(
        functools.partial(_proj_kernel, mode=mode, n_out=len(out_dtypes), scale=scale),
        out_shape=[jax.ShapeDtypeStruct((m, n), dt) for dt in out_dtypes],
        grid=(m // tm, n // tn),
        in_specs=in_specs,
        out_specs=[pl.BlockSpec((tm, tn), lambda i, j: (i, j)) for _ in out_dtypes],
        compiler_params=_cparams(("arbitrary", "arbitrary")),
        name="proj_" + mode,
    )(*args)
    return outs


def _outproj_kernel(a_ref, b_ref, wa_ref, wb_ref, x_ref, g_ref, xo_ref, ho_ref):
    y = x_ref[...] + _dot(a_ref[...], wa_ref[...]) + _dot(b_ref[...], wb_ref[...])
    xo_ref[...] = y
    ms = jnp.mean(y * y, axis=-1, keepdims=True)
    ho_ref[...] = (y * lax.rsqrt(ms + EPS) * g_ref[...]).astype(ho_ref.dtype)


def _outproj(a, b, w, x, gain, tm=512):
    m, d = x.shape
    ka, kb = a.shape[1], b.shape[1]
    tm = min(tm, m)
    return pl.pallas_call(
        _outproj_kernel,
        out_shape=[jax.ShapeDtypeStruct((m, d), F32), jax.ShapeDtypeStruct((m, d), BF16)],
        grid=(m // tm,),
        in_specs=[pl.BlockSpec((tm, ka), lambda i: (i, 0)),
                  pl.BlockSpec((tm, kb), lambda i: (i, 0)),
                  pl.BlockSpec((ka, d), lambda i: (0, 0)),
                  pl.BlockSpec((kb, d), lambda i: (0, 0)),
                  pl.BlockSpec((tm, d), lambda i: (i, 0)),
                  pl.BlockSpec((1, d), lambda i: (0, 0))],
        out_specs=[pl.BlockSpec((tm, d), lambda i: (i, 0)),
                   pl.BlockSpec((tm, d), lambda i: (i, 0))],
        compiler_params=_cparams(("arbitrary",)),
        name="outproj",
    )(a, b, w[:ka], w[ka:], x, gain.reshape(1, d))


def _ffn_up_kernel(h_ref, wu_ref, wg_ref, cw_ref, cb_ref, st_ref, a_ref, gl_ref, gbuf_ref, carry_ref,
                   *, tiles_per_seq, stepwise):
    i = pl.program_id(0)
    f = pl.program_id(1)
    tm = h_ref.shape[0]
    wu = wu_ref[...].astype(BF16)
    wg = wg_ref[...].astype(BF16)
    cw = cw_ref[...]
    cb = cb_ref[...]
    if stepwise:
        h = h_ref[...]
        u = _dot(h, wu)
        g = _dot(h, wg)
        gl_ref[...] = g
        gc = cb + cw[0:1, :] * st_ref[0] + cw[1:2, :] * st_ref[1] + cw[2:3, :] * g
        a_ref[...] = (_silu(gc) * u).astype(a_ref.dtype)
        return

    first = (i % tiles_per_seq) == 0

    @pl.when(first)
    def _():
        gbuf_ref[0:SUBLANES, :] = st_ref[0]

    @pl.when(jnp.logical_not(first))
    def _():
        gbuf_ref[0:SUBLANES, :] = carry_ref[f]

    tf = wu.shape[1]
    sub = min(tf, 256)
    h = h_ref[...]

    def up(c0):
        return _dot(h, wu[:, c0:c0 + sub]), _dot(h, wg[:, c0:c0 + sub])

    starts = list(range(0, tf, sub))
    ahead = up(starts[0])
    for n, c0 in enumerate(starts):
        u, g = ahead
        if n + 1 < len(starts):
            ahead = up(starts[n + 1])
        cols = slice(c0, c0 + sub)
        gbuf_ref[SUBLANES:SUBLANES + tm, cols] = g
        gm1 = gbuf_ref[SUBLANES - 1:SUBLANES - 1 + tm, cols]
        gm2 = gbuf_ref[SUBLANES - 2:SUBLANES - 2 + tm, cols]
        gc = cb[:, cols] + cw[0:1, cols] * gm2 + cw[1:2, cols] * gm1 + cw[2:3, cols] * g
        a_ref[:, cols] = (_silu(gc) * u).astype(a_ref.dtype)
    tail = gbuf_ref[tm:tm + SUBLANES, :]
    carry_ref[f] = tail
    gl_ref[0] = tail


def _ffn_down_kernel(a_ref, wd_ref, x_ref, xo_ref):
    xo_ref[...] = x_ref[...] + _dot(a_ref[...], wd_ref[...].astype(BF16))


def _ffn(x, h, w_up, w_down, layer, cw, cb, state, *, seq_len, stepwise, tm=1024, tf=512, tn=256):
    m, d = x.shape
    ff = w_down.shape[1]
    tm = min(tm, m)
    tf = min(tf, ff)
    tn = min(tn, d)
    n_i, n_f = m // tm, ff // tf
    tiles_per_seq = max(seq_len // tm, 1)
    if stepwise:
        st_spec = pl.BlockSpec((2, tm, tf), lambda i, f: (0, i, f))
        gl_shape = jax.ShapeDtypeStruct((m, ff), F32)
        gl_spec = pl.BlockSpec((tm, tf), lambda i, f: (i, f))
    else:
        st_spec = pl.BlockSpec((1, SUBLANES, tf), lambda i, f: (i // tiles_per_seq, 0, f))
        gl_shape = jax.ShapeDtypeStruct((n_i, SUBLANES, ff), F32)
        gl_spec = pl.BlockSpec((1, SUBLANES, tf), lambda i, f: (i, 0, f))
    a, gl = pl.pallas_call(
        functools.partial(_ffn_up_kernel, tiles_per_seq=tiles_per_seq, stepwise=stepwise),
        out_shape=[jax.ShapeDtypeStruct((m, ff), BF16), gl_shape],
        grid=(n_i, n_f),
        in_specs=[pl.BlockSpec((tm, d), lambda i, f: (i, 0)),
                  pl.BlockSpec((None, d, tf), lambda i, f: (layer, 0, f)),
                  pl.BlockSpec((None, d, tf), lambda i, f: (layer, 0, n_f + f)),
                  pl.BlockSpec((CONV_W, tf), lambda i, f: (0, f)),
                  pl.BlockSpec((1, tf), lambda i, f: (0, f)),
                  st_spec],
        out_specs=[pl.BlockSpec((tm, tf), lambda i, f: (i, f)), gl_spec],
        scratch_shapes=[pltpu.VMEM((SUBLANES + tm, tf), F32),
                        pltpu.VMEM((n_f, SUBLANES, tf), F32)],
        compiler_params=_cparams(("arbitrary", "arbitrary")),
        name="ffn_up_step" if stepwise else "ffn_up",
    )(h, w_up, w_up, cw, cb.reshape(1, ff), state)
    xo = pl.pallas_call(
        _ffn_down_kernel,
        out_shape=jax.ShapeDtypeStruct((m, d), F32),
        grid=(n_i, d // tn),
        in_specs=[pl.BlockSpec((tm, ff), lambda i, n: (i, 0)),
                  pl.BlockSpec((None, ff, tn), lambda i, n: (layer, 0, n)),
                  pl.BlockSpec((tm, tn), lambda i, n: (i, n))],
        out_specs=pl.BlockSpec((tm, tn), lambda i, n: (i, n)),
        compiler_params=_cparams(("arbitrary", "arbitrary")),
        name="ffn_down",
    )(a, w_down, x)
    return xo, gl


def _bisect_threshold(count_ge, lo, hi, k, max_iter=80):
    def cond(c):
        it, _, _, done = c
        return jnp.logical_and(it < max_iter, jnp.min(done) == 0)

    def body(c):
        it, lo, hi, done = c
        mid = 0.5 * (lo + hi)
        cnt = count_ge(mid)
        ge = cnt >= k
        stuck = jnp.logical_or(mid <= lo, mid >= hi)
        new_done = jnp.logical_or(jnp.logical_and(ge, cnt == k), stuck)
        lo = jnp.where(ge, mid, lo)
        hi = jnp.where(ge, hi, mid)
        return it + 1, lo, hi, jnp.maximum(done, new_done.astype(I32))

    done0 = (count_ge(lo) == k).astype(I32)
    _, lo, _, _ = lax.while_loop(cond, body, (jnp.int32(0), lo, hi, done0))
    return lo


def _dsa_prompt_kernel(q_ref, k_ref, v_ref, qi_ref, kia_ref, kib_ref, sm_ref, o_ref, sc_ref, s_ref,
                       *, tq, ts, topk):
    i = pl.program_id(1)
    q0 = i * tq
    nk = (q0 + tq + ts - 1) // ts
    qpos = q0 + _iota((tq, 1), 0)
    sm = sm_ref[...]
    wcols = [sm[:, IDX_DIM + h:IDX_DIM + h + 1] for h in range(IDX_HEADS)]

    def score_tile(kt, carry):
        ks = pl.multiple_of(kt * ts, ts)
        ka = kia_ref[pl.ds(ks, ts), :]
        kb = kib_ref[pl.ds(ks, ts), :]
        acc = jnp.zeros((tq, ts), F32)
        for p in range(IDX_HEADS // 2):
            qp = qi_ref[:, p * LANES:(p + 1) * LANES]
            acc = acc + wcols[2 * p] * jnp.maximum(_dot_nt(qp, ka), 0.0)
            acc = acc + wcols[2 * p + 1] * jnp.maximum(_dot_nt(qp, kb), 0.0)
        kpos = ks + _iota((1, ts), 1)
        sc_ref[kt] = jnp.where(kpos <= qpos, acc * (IDX_HEADS ** -0.5), NEG)
        return carry

    lax.fori_loop(0, nk, score_tile, 0)

    def reduce_tiles(fn, comb, init):
        def body(kt, c):
            x = fn(sc_ref[kt])
            for j in range(ts // LANES):
                c = comb(c, x[:, j * LANES:(j + 1) * LANES])
            return c
        return lax.fori_loop(0, nk, body, jnp.full((tq, LANES), init, F32))

    def count_ge(t):
        part = reduce_tiles(lambda s: jnp.where(s >= t, 1.0, 0.0), lambda a, b: a + b, 0.0)
        return jnp.sum(part, axis=1, keepdims=True)

    hi = jnp.max(reduce_tiles(lambda s: s, jnp.maximum, NEG), axis=1, keepdims=True)
    lo = jnp.min(reduce_tiles(lambda s: jnp.where(s > 0.5 * NEG, s, -NEG), jnp.minimum, -NEG),
                 axis=1, keepdims=True)
    thr = _bisect_threshold(count_ge, lo, hi, float(topk))
    thr = jnp.where(qpos + 1 <= topk, NEG, thr)

    def to_bias(kt, carry):
        s = sc_ref[kt]
        keep = jnp.logical_and(s >= thr, s > 0.5 * NEG)
        sc_ref[kt] = jnp.where(keep, 0.0, NEG)
        return carry

    lax.fori_loop(0, nk, to_bias, 0)

    scale2 = (A_DH ** -0.5) * 1.4426950408889634
    rep = A_HEADS // A_KV
    for g in range(A_KV):
        qg = jnp.concatenate(
            [q_ref[:, (g * rep + r) * A_DH:(g * rep + r + 1) * A_DH] for r in range(rep)], axis=0)

        def scores(kt, g=g, qg=qg):
            ks = pl.multiple_of(kt * ts, ts)
            kk = k_ref[pl.ds(ks, ts), g * A_DH:(g + 1) * A_DH]
            return _dot_nt(qg, kk) * scale2 + jnp.concatenate([sc_ref[kt]] * rep, axis=0)

        def lane_fold(x, part, comb):
            for j in range(ts // LANES):
                part = comb(part, x[:, j * LANES:(j + 1) * LANES])
            return part

        def pass_max(kt, mpart):
            s = scores(kt)
            s_ref[kt] = s
            return lane_fold(s, mpart, jnp.maximum)

        mpart = lax.fori_loop(0, nk, pass_max, jnp.full((rep * tq, LANES), NEG, F32))
        m = jnp.max(mpart, axis=1, keepdims=True)

        def pass_acc(kt, c, g=g, m=m):
            lpart, acc = c
            ks = pl.multiple_of(kt * ts, ts)
            vv = v_ref[pl.ds(ks, ts), g * A_DH:(g + 1) * A_DH]
            p = jnp.exp2(s_ref[kt] - m)
            return lane_fold(p, lpart, lambda a, b: a + b), acc + _dot(p.astype(BF16), vv)

        lpart, acc = lax.fori_loop(0, nk, pass_acc, (jnp.zeros((rep * tq, LANES), F32),
                                                     jnp.zeros((rep * tq, A_DH), F32)))
        o = acc / jnp.sum(lpart, axis=1, keepdims=True)
        for r in range(rep):
            o_ref[:, (g * rep + r) * A_DH:(g * rep + r + 1) * A_DH] = (
                o[r * tq:(r + 1) * tq].astype(o_ref.dtype))


def _dsa_prompt(q, k, v, qi, kia, kib, small, *, batch, seq, tq=256, ts=512):
    topk = min(TOPK, seq // 4)
    nq = seq // tq

    def rows(width):
        return pl.BlockSpec((tq, width), lambda b, i: (b * nq + i, 0))

    def whole(width):
        return pl.BlockSpec((seq, width), lambda b, i: (b, 0))

    return pl.pallas_call(
        functools.partial(_dsa_prompt_kernel, tq=tq, ts=ts, topk=topk),
        out_shape=jax.ShapeDtypeStruct((batch * seq, A_HEADS * A_DH), BF16),
        grid=(batch, nq),
        in_specs=[rows(A_HEADS * A_DH), whole(A_KV * A_DH), whole(A_KV * A_DH),
                  rows(IDX_HEADS * IDX_DIM), whole(LANES), whole(LANES), rows(LANES)],
        out_specs=rows(A_HEADS * A_DH),
        scratch_shapes=[pltpu.VMEM((seq // ts, tq, ts), F32),
                        pltpu.VMEM((seq // ts, (A_HEADS // A_KV) * tq, ts), F32)],
        compiler_params=_cparams(("arbitrary", "arbitrary")),
        name="dsa_prompt",
    )(q, k, v, qi, kia, kib, small)


def _dsa_score_kernel(pt_ref, qi_ref, wi_ref, kinew_ref, pool_ref, sc_ref, snew_ref,
                      kbuf_ref, sem_ref, *, layer, n_pages):
    b = pl.program_id(0)
    nb = pl.num_programs(0)
    past = n_pages * PAGE

    def page_copy(bb, p, slot):
        return pltpu.make_async_copy(pool_ref.at[layer, pt_ref[bb, p]],
                                     kbuf_ref.at[slot, :, pl.ds(p * PAGE, PAGE)], sem_ref.at[slot])

    def start_all(bb, slot):
        def body(p, c):
            page_copy(bb, p, slot).start()
            return c
        lax.fori_loop(0, n_pages, body, 0, unroll=8)

    def wait_all(bb, slot):
        def body(p, c):
            page_copy(bb, p, slot).wait()
            return c
        lax.fori_loop(0, n_pages, body, 0, unroll=8)

    slot = b % 2

    @pl.when(b == 0)
    def _():
        start_all(b, slot)

    @pl.when(b + 1 < nb)
    def _():
        start_all(b + 1, 1 - slot)

    wait_all(b, slot)

    qi = qi_ref[0]
    w = wi_ref[0]
    chunk_pages = min(16, n_pages)
    for c in range(n_pages // chunk_pages):
        lanes = slice(c * chunk_pages * PAGE, (c + 1) * chunk_pages * PAGE)
        kt = kbuf_ref[slot, :, lanes].astype(BF16)
        lg = jnp.maximum(_dot(qi, kt), 0.0) * w
        row = jnp.sum(lg, axis=0, keepdims=True) * (IDX_HEADS ** -0.5)
        for p in range(chunk_pages):
            sc_ref[0, c * chunk_pages + p:c * chunk_pages + p + 1, :] = row[:, p * PAGE:(p + 1) * PAGE]

    kin = kinew_ref[0].astype(BF16).astype(F32)
    lnew = jnp.sum(qi.astype(F32) * kin, axis=1, keepdims=True)
    snew = jnp.sum(jnp.maximum(lnew, 0.0) * w, axis=0, keepdims=True) * (IDX_HEADS ** -0.5)
    snew_ref[0] = jnp.broadcast_to(snew, (1, LANES))


def _dsa_pick_kernel(dense_ref, snew_ref, sc_ref, ptv_ref, idx_ref, thr_ref, *, n_pages, topk):
    b = pl.program_id(0)
    past = n_pages * PAGE

    @pl.when(b == 0)
    def _():
        bd = dense_ref.shape[0]
        s_new = snew_ref[:, 0:1]

        def reduce_all(fn, comb, init):
            c = jnp.full((bd, LANES), init, F32)
            for j in range(past // LANES):
                c = comb(c, fn(dense_ref[:, j * LANES:(j + 1) * LANES]))
            return c

        def count_ge(t):
            part = reduce_all(lambda s: jnp.where(s >= t, 1.0, 0.0), lambda a, b_: a + b_, 0.0)
            return jnp.sum(part, axis=1, keepdims=True) + jnp.where(s_new >= t, 1.0, 0.0)

        hi = jnp.maximum(jnp.max(reduce_all(lambda s: s, jnp.maximum, NEG), axis=1, keepdims=True), s_new)
        lo = jnp.minimum(jnp.min(reduce_all(lambda s: s, jnp.minimum, -NEG), axis=1, keepdims=True), s_new)
        thr_all = _bisect_threshold(count_ge, lo, hi, float(topk))
        thr_ref[...] = jnp.broadcast_to(thr_all, thr_ref.shape)

    thr = thr_ref[pl.ds(b, 1), 0:1]
    snew = snew_ref[pl.ds(b, 1), 0:1]
    sc = sc_ref[0]

    def total(x):
        return jnp.sum(jnp.sum(x, axis=1, keepdims=True), axis=0, keepdims=True)

    gt = sc > thr
    eq = sc == thr
    n_gt = total(jnp.where(gt, 1.0, 0.0)) + jnp.where(snew > thr, 1.0, 0.0)
    need = topk - n_gt
    u_incl = (_iota((PAGE, PAGE), 0) <= _iota((PAGE, PAGE), 1)).astype(BF16)
    pg = (n_pages, n_pages)
    l_strict = (_iota(pg, 1) < _iota(pg, 0)).astype(BF16)
    u_pages = (_iota(pg, 0) <= _iota(pg, 1)).astype(BF16)
    eqf = eq.astype(BF16)
    eq_rank = _dot(eqf, u_incl) + jnp.sum(_dot(l_strict, eqf), axis=1, keepdims=True)
    sel = jnp.logical_or(gt, jnp.logical_and(eq, eq_rank <= need))
    self_ = sel.astype(BF16)
    inrow = _dot(self_, u_incl)
    ones8 = jnp.ones((SUBLANES, PAGE), BF16)
    crow = _dot_nt(ones8, self_)
    cend = _dot(crow.astype(BF16), u_pages)
    cstart = cend - crow
    slots = topk
    r = (_iota((slots, 1), 0) + 1).astype(F32)
    onehot = jnp.logical_and(cstart[0:1, :] < r, r <= cend[0:1, :])
    ohf = onehot.astype(F32)
    rowsel = _dot(onehot.astype(BF16), (sel.astype(F32) * inrow).astype(BF16))
    qr = r - jnp.sum(ohf * cstart[0:1, :], axis=1, keepdims=True)
    lane = _iota((slots, PAGE), 1)
    off = jnp.sum(jnp.where(rowsel == qr, lane, 0), axis=1, keepdims=True)
    page = jnp.sum(jnp.where(onehot, _iota((slots, n_pages), 1), 0), axis=1, keepdims=True)
    has = jnp.sum(ohf, axis=1, keepdims=True) > 0.5
    phys = jnp.sum(ohf * ptv_ref[0].astype(F32), axis=1, keepdims=True).astype(I32)
    idx_ref[0] = jnp.where(has, phys * PAGE + off, -1)


def _dsa_select(page_table, qi, wi, ki_new, pool_ki_t, *, layer):
    bd, n_pages = page_table.shape
    past = n_pages * PAGE
    topk = min(TOPK, (past + 1) // 4)
    grid_spec = pltpu.PrefetchScalarGridSpec(
        num_scalar_prefetch=1,
        grid=(bd,),
        in_specs=[pl.BlockSpec((1, IDX_HEADS, IDX_DIM), lambda b, pt: (b, 0, 0)),
                  pl.BlockSpec((1, IDX_HEADS, 1), lambda b, pt: (b, 0, 0)),
                  pl.BlockSpec((1, 1, IDX_DIM), lambda b, pt: (b, 0, 0)),
                  pl.BlockSpec(memory_space=pl.ANY)],
        out_specs=[pl.BlockSpec((1, n_pages, PAGE), lambda b, pt: (b, 0, 0)),
                   pl.BlockSpec((1, 1, LANES), lambda b, pt: (b, 0, 0))],
        scratch_shapes=[pltpu.VMEM((2, IDX_DIM, past), F32),
                        pltpu.SemaphoreType.DMA((2,))],
    )
    scores, snew = pl.pallas_call(
        functools.partial(_dsa_score_kernel, layer=layer, n_pages=n_pages),
        out_shape=[jax.ShapeDtypeStruct((bd, n_pages, PAGE), F32),
                   jax.ShapeDtypeStruct((bd, 1, LANES), F32)],
        grid_spec=grid_spec,
        compiler_params=pltpu.CompilerParams(dimension_semantics=("arbitrary",),
                                             vmem_limit_bytes=VMEM_LIMIT, disable_bounds_checks=True),
        name="dsa_score",
    )(page_table, qi, wi, ki_new, pool_ki_t)
    return pl.pallas_call(
        functools.partial(_dsa_pick_kernel, n_pages=n_pages, topk=topk),
        out_shape=jax.ShapeDtypeStruct((bd, topk, 1), I32),
        grid=(bd,),
        in_specs=[pl.BlockSpec((bd, past), lambda b: (0, 0)),
                  pl.BlockSpec((bd, LANES), lambda b: (0, 0)),
                  pl.BlockSpec((1, n_pages, PAGE), lambda b: (b, 0, 0)),
                  pl.BlockSpec((1, 1, n_pages), lambda b: (b, 0, 0))],
        out_specs=pl.BlockSpec((1, topk, 1), lambda b: (b, 0, 0)),
        scratch_shapes=[pltpu.VMEM((bd, LANES), F32)],
        compiler_params=_cparams(("arbitrary",)),
        name="dsa_pick",
    )(scores.reshape(bd, past), snew.reshape(bd, LANES), scores, page_table.reshape(bd, 1, n_pages))


def _dsa_gather_attend_kernel(idx_s, q_ref, idxv_ref, knew_ref, vnew_ref, kpool_ref, vpool_ref,
                              o_ref, kbuf_ref, vbuf_ref, sem_ref, *, layer, topk):
    b = pl.program_id(0)
    nb = pl.num_programs(0)

    def copies(bb, r, slot):
        row = jnp.maximum(idx_s[bb, r], 0)
        return (pltpu.make_async_copy(kpool_ref.at[layer, row], kbuf_ref.at[slot, r],
                                      sem_ref.at[0, slot]),
                pltpu.make_async_copy(vpool_ref.at[layer, row], vbuf_ref.at[slot, r],
                                      sem_ref.at[1, slot]))

    def start_all(bb, slot):
        def body(r, c):
            ck, cv = copies(bb, r, slot)
            ck.start()
            cv.start()
            return c
        lax.fori_loop(0, topk, body, 0, unroll=8)

    def wait_all(bb, slot):
        def body(r, c):
            ck, cv = copies(bb, r, slot)
            ck.wait()
            cv.wait()
            return c
        lax.fori_loop(0, topk, body, 0, unroll=8)

    slot = b % 2

    @pl.when(b == 0)
    def _():
        start_all(b, slot)

    @pl.when(b + 1 < nb)
    def _():
        start_all(b + 1, 1 - slot)

    wait_all(b, slot)

    q = q_ref[0]
    in_past = idxv_ref[0] >= 0
    rep = A_HEADS // A_KV
    head_group = _iota((A_HEADS, A_DH), 0) // rep
    out = jnp.zeros((A_HEADS, A_DH), F32)
    for g in range(A_KV):
        kg = jnp.where(in_past, kbuf_ref[slot, :, g, :], knew_ref[0, g:g + 1, :]).astype(BF16)
        vg = jnp.where(in_past, vbuf_ref[slot, :, g, :], vnew_ref[0, g:g + 1, :]).astype(BF16)
        s = _dot_nt(q, kg) * (A_DH ** -0.5)
        m = jnp.max(s, axis=1, keepdims=True)
        p = jnp.exp(s - m)
        p = p / jnp.sum(p, axis=1, keepdims=True)
        o = _dot(p.astype(BF16), vg)
        out = jnp.where(head_group == g, o, out)
    o_ref[0] = out.astype(o_ref.dtype)


def _dsa_gather_attend(idx, q, k_new, v_new, pool_k, pool_v, *, layer):
    bd, topk = idx.shape[:2]
    grid_spec = pltpu.PrefetchScalarGridSpec(
        num_scalar_prefetch=1,
        grid=(bd,),
        in_specs=[pl.BlockSpec((1, A_HEADS, A_DH), lambda b, i: (b, 0, 0)),
                  pl.BlockSpec((1, topk, 1), lambda b, i: (b, 0, 0)),
                  pl.BlockSpec((1, A_KV, A_DH), lambda b, i: (b, 0, 0)),
                  pl.BlockSpec((1, A_KV, A_DH), lambda b, i: (b, 0, 0)),
                  pl.BlockSpec(memory_space=pl.ANY),
                  pl.BlockSpec(memory_space=pl.ANY)],
        out_specs=pl.BlockSpec((1, A_HEADS, A_DH), lambda b, i: (b, 0, 0)),
        scratch_shapes=[pltpu.VMEM((2, topk, A_KV, A_DH), F32),
                        pltpu.VMEM((2, topk, A_KV, A_DH), F32),
                        pltpu.SemaphoreType.DMA((2, 2))],
    )
    return pl.pallas_call(
        functools.partial(_dsa_gather_attend_kernel, layer=layer, topk=topk),
        out_shape=jax.ShapeDtypeStruct((bd, A_HEADS, A_DH), BF16),
        grid_spec=grid_spec,
        compiler_params=pltpu.CompilerParams(dimension_semantics=("arbitrary",),
                                             vmem_limit_bytes=VMEM_LIMIT, disable_bounds_checks=True),
        name="dsa_gather_attend",
    )(idx.reshape(bd, topk), q, idx, k_new, v_new,
      pool_k.reshape(pool_k.shape[0], -1, A_KV, A_DH), pool_v.reshape(pool_v.shape[0], -1, A_KV, A_DH))


def _head_rms_gate(o, gain, gate):
    ms = jnp.mean(o * o, axis=-1, keepdims=True)
    return o * lax.rsqrt(ms + EPS) * gain * _silu(gate)


def _gla_prompt_kernel(qkr_ref, v_ref, sm_ref, w2_ref, b2_ref, gn_ref, o_ref, sfin_ref, s_ref,
                       *, tt, chunk):
    i = pl.program_id(1)
    ni = pl.num_programs(1)

    @pl.when(i == 0)
    def _():
        s_ref[...] = jnp.zeros_like(s_ref)

    hdk = B_HEADS * B_DK
    r_i = _iota((chunk, chunk), 0)
    c_i = _iota((chunk, chunk), 1)
    causal = c_i <= r_i
    tril = causal.astype(BF16)
    x = _dot(sm_ref[...].astype(BF16), w2_ref[...]) + b2_ref[...]
    log_a = _log_sigmoid(x) * (1.0 / B_TAU)
    for c in range(tt // chunk):
        rows = slice(c * chunk, (c + 1) * chunk)
        for h in range(B_HEADS):
            dk = slice(h * B_DK, (h + 1) * B_DK)
            dv = slice(h * B_DV, (h + 1) * B_DV)
            q = qkr_ref[rows, dk] * (B_DK ** -0.5)
            k = qkr_ref[rows, hdk + h * B_DK:hdk + (h + 1) * B_DK]
            gate = qkr_ref[rows, 2 * hdk + h * B_DV:2 * hdk + (h + 1) * B_DV]
            v = v_ref[rows, dv]
            bc = _dot_exact_lhs(tril, log_a[rows, dk])
            bl = bc[chunk - 1:chunk, :]
            qd = (q * jnp.exp(bc)).astype(BF16)
            kd = (k * jnp.exp(-bc)).astype(BF16)
            att = jnp.where(causal, _dot_nt(qd, kd), 0.0)
            s_old = s_ref[h]
            o = _dot(qd, s_old.astype(BF16)) + _dot(att.astype(BF16), v)
            kdec = k * jnp.exp(bl - bc)
            decay = jnp.broadcast_to(jnp.exp(bl), (chunk, B_DK)).T
            decay = jnp.concatenate([decay] * (B_DV // B_DK), axis=1)
            s_ref[h] = decay * s_old + _dot(kdec.T.astype(BF16), v)
            o_ref[rows, dv] = _head_rms_gate(o, gn_ref[...], gate).astype(o_ref.dtype)

    @pl.when(i == ni - 1)
    def _():
        sfin_ref[0] = s_ref[...]


def _gla_prompt(qkr, v, small, w2pad, b2, gn, *, batch, seq, tt=256, chunk=128):
    nt = seq // tt
    hdk, hdv = B_HEADS * B_DK, B_HEADS * B_DV

    def rows(width):
        return pl.BlockSpec((tt, width), lambda b, i: (b * nt + i, 0))

    def const(shape):
        return pl.BlockSpec(shape, lambda b, i: (0,) * len(shape))

    return pl.pallas_call(
        functools.partial(_gla_prompt_kernel, tt=tt, chunk=chunk),
        out_shape=[jax.ShapeDtypeStruct((batch * seq, hdv), BF16),
                   jax.ShapeDtypeStruct((batch, B_HEADS, B_DK, B_DV), F32)],
        grid=(batch, nt),
        in_specs=[rows(2 * hdk + hdv), rows(hdv), rows(LANES),
                  const((LANES, hdk)), const((1, hdk)), const((1, B_DV))],
        out_specs=[rows(hdv),
                   pl.BlockSpec((1, B_HEADS, B_DK, B_DV), lambda b, i: (b, 0, 0, 0))],
        scratch_shapes=[pltpu.VMEM((B_HEADS, B_DK, B_DV), F32)],
        compiler_params=_cparams(("arbitrary", "arbitrary")),
        name="gla_prompt",
    )(qkr, v, small, w2pad, b2.reshape(1, hdk), gn.reshape(1, B_DV))


def _row_to_col(row):
    n = row.shape[1]
    eye = _iota((n, n), 0) == _iota((n, n), 1)
    return jnp.sum(jnp.where(eye, jnp.broadcast_to(row, (n, n)), 0.0), axis=1, keepdims=True)


def _gla_step_kernel(qkr_ref, v_ref, sm_ref, w2_ref, b2_ref, gn_ref, s_ref, o_ref, snew_ref):
    hdk = B_HEADS * B_DK
    smb = jnp.broadcast_to(sm_ref[0], (SUBLANES, LANES)).astype(BF16)
    x = _dot(smb, w2_ref[...])[0:1, :] + b2_ref[...]
    log_a = _log_sigmoid(x) * (1.0 / B_TAU)
    for h in range(B_HEADS):
        dk = slice(h * B_DK, (h + 1) * B_DK)
        dv = slice(h * B_DV, (h + 1) * B_DV)
        g_row = log_a[:, dk]
        q_row = qkr_ref[0, :, dk] * (B_DK ** -0.5)
        k_row = qkr_ref[0, :, hdk + h * B_DK:hdk + (h + 1) * B_DK]
        gate = qkr_ref[0, :, 2 * hdk + h * B_DV:2 * hdk + (h + 1) * B_DV]
        v_row = v_ref[0, :, dv].astype(F32)
        qd_row = q_row * jnp.exp(g_row)
        kd_row = k_row * jnp.exp(-g_row)
        s_old = s_ref[0, h]
        snew_ref[0, h] = _row_to_col(jnp.exp(g_row)) * s_old + _row_to_col(k_row) * v_row
        att = jnp.sum(qd_row * kd_row, axis=1, keepdims=True)
        o = jnp.sum(_row_to_col(qd_row) * s_old, axis=0, keepdims=True) + att * v_row
        o_ref[0, :, dv] = _head_rms_gate(o, gn_ref[...], gate).astype(o_ref.dtype)


def _gla_step(qkr, v, small, w2pad, b2, gn, state):
    bd = qkr.shape[0]
    hdk, hdv = B_HEADS * B_DK, B_HEADS * B_DV

    def row(width):
        return pl.BlockSpec((1, 1, width), lambda b: (b, 0, 0))

    def const(shape):
        return pl.BlockSpec(shape, lambda b: (0,) * len(shape))

    st = pl.BlockSpec((1, B_HEADS, B_DK, B_DV), lambda b: (b, 0, 0, 0))
    return pl.pallas_call(
        _gla_step_kernel,
        out_shape=[jax.ShapeDtypeStruct((bd, 1, hdv), BF16),
                   jax.ShapeDtypeStruct((bd, B_HEADS, B_DK, B_DV), F32)],
        grid=(bd,),
        in_specs=[row(2 * hdk + hdv), row(hdv), row(LANES),
                  const((LANES, hdk)), const((1, hdk)), const((1, B_DV)), st],
        out_specs=[row(hdv), st],
        compiler_params=_cparams(("arbitrary",)),
        name="gla_step",
    )(qkr.reshape(bd, 1, -1), v.reshape(bd, 1, -1), small.reshape(bd, 1, -1),
      w2pad, b2.reshape(1, hdk), gn.reshape(1, B_DV), state)


def _head_rms_sig(h, gain, gate):
    ms = jnp.mean(h * h, axis=-1, keepdims=True)
    return h * lax.rsqrt(ms + EPS) * gain * _sigmoid(gate)


def _mlstm_prompt_kernel(qko_ref, v_ref, sm_ref, bias_ref, hn_ref, o_ref, cfin_ref, nfin_ref, mfin_ref,
                         c_ref, n_ref, m_ref, *, tt, chunk):
    i = pl.program_id(1)
    ni = pl.num_programs(1)

    @pl.when(i == 0)
    def _():
        c_ref[...] = jnp.zeros_like(c_ref)
        n_ref[...] = jnp.zeros_like(n_ref)
        m_ref[...] = jnp.zeros_like(m_ref)

    hdk = C_HEADS * C_DK
    r_i = _iota((chunk, chunk), 0)
    c_i = _iota((chunk, chunk), 1)
    causal = c_i <= r_i
    tril = causal.astype(BF16)
    pre = sm_ref[...] + bias_ref[...]
    for c in range(tt // chunk):
        rows = slice(c * chunk, (c + 1) * chunk)
        for h in range(C_HEADS):
            dk = slice(h * C_DK, (h + 1) * C_DK)
            dv = slice(h * C_DV, (h + 1) * C_DV)
            icol = pre[rows, h:h + 1]
            fcol = _log_sigmoid(pre[rows, C_HEADS + h:C_HEADS + h + 1])
            bmat = _dot_exact_lhs(tril, jnp.broadcast_to(fcol, (chunk, chunk)))
            bcol = bmat[:, 0:1]
            b_t = bmat.T
            i_t = jnp.broadcast_to(icol, (chunk, chunk)).T
            m_prev = m_ref[h, 0:1, 0:1]
            log_d = jnp.where(causal, bmat - b_t + i_t, NEG)
            inter = bcol + m_prev
            m_t = jnp.maximum(inter, jnp.max(log_d, axis=1, keepdims=True))
            dmat = jnp.exp(log_d - m_t)
            g = jnp.exp(inter - m_t)
            q = qko_ref[rows, dk]
            ks = qko_ref[rows, hdk + h * C_DK:hdk + (h + 1) * C_DK] * (C_DK ** -0.5)
            gate = qko_ref[rows, 2 * hdk + h * C_DV:2 * hdk + (h + 1) * C_DV]
            v = v_ref[rows, dv]
            qb = q.astype(BF16)
            qk = _dot_nt(qb, ks.astype(BF16)) * dmat
            c_old = c_ref[h]
            n_old = n_ref[h, 0:1, :]
            num = g * _dot(qb, c_old.astype(BF16)) + _dot(qk.astype(BF16), v)
            den = g * jnp.sum(q * n_old, axis=1, keepdims=True) + jnp.sum(qk, axis=1, keepdims=True)
            hh = num / jnp.maximum(jnp.abs(den), jnp.exp(-m_t))
            o_ref[rows, dv] = _head_rms_sig(hh, hn_ref[...], gate).astype(o_ref.dtype)
            m_last = m_t[chunk - 1:chunk, :]
            g_last = g[chunk - 1:chunk, :]
            wcol = jnp.exp(bcol[chunk - 1:chunk, :] - bcol + icol - m_last)
            kw = ks * wcol
            c_ref[h] = g_last * c_old + _dot(kw.T.astype(BF16), v)
            n_ref[h] = jnp.broadcast_to(g_last * n_old + jnp.sum(kw, axis=0, keepdims=True),
                                        (SUBLANES, C_DK))
            m_ref[h] = jnp.broadcast_to(m_last, (SUBLANES, LANES))

    @pl.when(i == ni - 1)
    def _():
        cfin_ref[0] = c_ref[...]
        nfin_ref[0] = n_ref[...]
        mfin_ref[0] = m_ref[...]


def _mlstm_prompt(qko, v, small, bias, hn, *, batch, seq, tt=256, chunk=128):
    nt = seq // tt
    hdk, hdv = C_HEADS * C_DK, C_HEADS * C_DV

    def rows(width):
        return pl.BlockSpec((tt, width), lambda b, i: (b * nt + i, 0))

    def const(shape):
        return pl.BlockSpec(shape, lambda b, i: (0,) * len(shape))

    def per_batch(shape):
        return pl.BlockSpec((1,) + shape, lambda b, i: (b,) + (0,) * len(shape))

    return pl.pallas_call(
        functools.partial(_mlstm_prompt_kernel, tt=tt, chunk=chunk),
        out_shape=[jax.ShapeDtypeStruct((batch * seq, hdv), BF16),
                   jax.ShapeDtypeStruct((batch, C_HEADS, C_DK, C_DV), F32),
                   jax.ShapeDtypeStruct((batch, C_HEADS, SUBLANES, C_DK), F32),
                   jax.ShapeDtypeStruct((batch, C_HEADS, SUBLANES, LANES), F32)],
        grid=(batch, nt),
        in_specs=[rows(2 * hdk + hdv), rows(hdv), rows(LANES), const((1, LANES)), const((1, C_DV))],
        out_specs=[rows(hdv), per_batch((C_HEADS, C_DK, C_DV)),
                   per_batch((C_HEADS, SUBLANES, C_DK)), per_batch((C_HEADS, SUBLANES, LANES))],
        scratch_shapes=[pltpu.VMEM((C_HEADS, C_DK, C_DV), F32),
                        pltpu.VMEM((C_HEADS, SUBLANES, C_DK), F32),
                        pltpu.VMEM((C_HEADS, SUBLANES, LANES), F32)],
        compiler_params=_cparams(("arbitrary", "arbitrary")),
        name="mlstm_prompt",
    )(qko, v, small, bias, hn.reshape(1, C_DV))


def _mlstm_step_kernel(qko_ref, v_ref, sm_ref, bias_ref, hn_ref, c_ref, n_ref, m_ref,
                       o_ref, cnew_ref, nnew_ref, mnew_ref):
    hdk = C_HEADS * C_DK
    pre = sm_ref[0] + bias_ref[...]
    m_all = m_ref[0]
    m_out = jnp.zeros_like(m_all)
    for h in range(C_HEADS):
        dk = slice(h * C_DK, (h + 1) * C_DK)
        dv = slice(h * C_DV, (h + 1) * C_DV)
        i_pre = pre[:, h:h + 1]
        log_f = _log_sigmoid(pre[:, C_HEADS + h:C_HEADS + h + 1])
        m_prev = m_all[:, h:h + 1]
        inter = log_f + m_prev
        m_t = jnp.maximum(inter, i_pre)
        d = jnp.exp(i_pre - m_t)
        g = jnp.exp(inter - m_t)
        q_row = qko_ref[0, :, dk]
        ks_row = qko_ref[0, :, hdk + h * C_DK:hdk + (h + 1) * C_DK] * (C_DK ** -0.5)
        gate = qko_ref[0, :, 2 * hdk + h * C_DV:2 * hdk + (h + 1) * C_DV]
        v_row = v_ref[0, :, dv].astype(F32)
        c_old = c_ref[0, h]
        n_old = n_ref[0, h:h + 1, :]
        qk = jnp.sum(q_row * ks_row, axis=1, keepdims=True) * d
        num = g * jnp.sum(_row_to_col(q_row) * c_old, axis=0, keepdims=True) + qk * v_row
        den = g * jnp.sum(q_row * n_old, axis=1, keepdims=True) + qk
        hh = num / jnp.maximum(jnp.abs(den), jnp.exp(-m_t))
        o_ref[0, :, dv] = _head_rms_sig(hh, hn_ref[...], gate).astype(o_ref.dtype)
        cnew_ref[0, h] = g * c_old + _row_to_col(d * ks_row) * v_row
        nnew_ref[0, h:h + 1, :] = g * n_old + d * ks_row
        m_out = jnp.where(_iota(m_all.shape, 1) == h, m_t, m_out)
    mnew_ref[0] = m_out


def _mlstm_step(qko, v, small, bias, hn, c0, n0, m0):
    bd = qko.shape[0]
    hdk, hdv = C_HEADS * C_DK, C_HEADS * C_DV

    def row(width):
        return pl.BlockSpec((1, 1, width), lambda b: (b, 0, 0))

    def const(shape):
        return pl.BlockSpec(shape, lambda b: (0,) * len(shape))

    cs = pl.BlockSpec((1, C_HEADS, C_DK, C_DV), lambda b: (b, 0, 0, 0))
    ns = pl.BlockSpec((1, C_HEADS, C_DK), lambda b: (b, 0, 0))
    return pl.pallas_call(
        _mlstm_step_kernel,
        out_shape=[jax.ShapeDtypeStruct((bd, 1, hdv), BF16),
                   jax.ShapeDtypeStruct((bd, C_HEADS, C_DK, C_DV), F32),
                   jax.ShapeDtypeStruct((bd, C_HEADS, C_DK), F32),
                   jax.ShapeDtypeStruct((bd, 1, C_HEADS), F32)],
        grid=(bd,),
        in_specs=[row(2 * hdk + hdv), row(hdv), row(LANES), const((1, LANES)), const((1, C_DV)),
                  cs, ns, row(C_HEADS)],
        out_specs=[row(hdv), cs, ns, row(C_HEADS)],
        compiler_params=_cparams(("arbitrary",)),
        name="mlstm_step",
    )(qko.reshape(bd, 1, -1), v.reshape(bd, 1, -1), small.reshape(bd, 1, -1), bias,
      hn.reshape(1, C_DV), c0, n0, m0.reshape(bd, 1, C_HEADS))


def _swa_prompt_kernel(sink_ref, q_ref, kvp_ref, kvc_ref, o_ref, *, w):
    j = pl.program_id(1)
    band = jnp.concatenate([kvp_ref[...], kvc_ref[...]], axis=0)
    kk = band[:, :LANES]
    vv = band[:, LANES:]
    lane = _iota((2 * w, LANES), 1)
    qi = _iota((w, 2 * w), 0)
    kj = _iota((w, 2 * w), 1)
    allowed = jnp.logical_and(kj >= qi, kj <= qi + w)
    allowed = jnp.logical_and(allowed, jnp.logical_or(j > 0, kj >= w))
    rep = D_HEADS // D_KV
    pairs = rep // 2
    allowed = jnp.concatenate([allowed] * pairs, axis=0)
    zero = jnp.zeros_like(kk)
    for g in range(D_KV):
        own = (lane // D_DH) == g
        k_own = jnp.where(own, kk, zero)
        v_own = jnp.where(own, vv, zero)
        k_oth = pltpu.roll(k_own, D_DH, 1)
        v_oth = pltpu.roll(v_own, D_DH, 1)
        k_lo, k_hi = (k_own, k_oth) if g == 0 else (k_oth, k_own)
        v_lo, v_hi = (v_own, v_oth) if g == 0 else (v_oth, v_own)
        qs = jnp.concatenate([q_ref[:, (g * pairs + p) * LANES:(g * pairs + p + 1) * LANES]
                              for p in range(pairs)], axis=0)
        outs = []
        for half, (kh, vh) in enumerate(((k_lo, v_lo), (k_hi, v_hi))):
            kh = kh.astype(BF16)
            vh = vh.astype(BF16)
            s = _dot_nt(qs, kh) * (D_DH ** -0.5)
            s = jnp.where(allowed, s, NEG)
            sink = jnp.concatenate(
                [jnp.full((w, 1), sink_ref[2 * (g * pairs + p) + half], F32) for p in range(pairs)],
                axis=0)
            m = jnp.maximum(jnp.max(s, axis=1, keepdims=True), sink)
            e = jnp.where(allowed, jnp.exp(s - m), 0.0)
            p_ = e / (jnp.sum(e, axis=1, keepdims=True) + jnp.exp(sink - m))
            outs.append(_dot(p_.astype(BF16), vh))
        o = outs[0] + outs[1]
        for p in range(pairs):
            o_ref[:, (g * pairs + p) * LANES:(g * pairs + p + 1) * LANES] = (
                o[p * w:(p + 1) * w].astype(o_ref.dtype))


def _swa_prompt(q, kv, sinks, *, batch, seq):
    w = WINDOW
    nb = seq // w
    grid_spec = pltpu.PrefetchScalarGridSpec(
        num_scalar_prefetch=0,
        grid=(batch, nb),
        in_specs=[pl.BlockSpec(memory_space=pltpu.SMEM),
                  pl.BlockSpec((w, D_HEADS * D_DH), lambda b, j: (b * nb + j, 0)),
                  pl.BlockSpec((w, 2 * LANES), lambda b, j: (b * nb + jnp.maximum(j - 1, 0), 0)),
                  pl.BlockSpec((w, 2 * LANES), lambda b, j: (b * nb + j, 0))],
        out_specs=pl.BlockSpec((w, D_HEADS * D_DH), lambda b, j: (b * nb + j, 0)),
    )
    return pl.pallas_call(
        functools.partial(_swa_prompt_kernel, w=w),
        out_shape=jax.ShapeDtypeStruct((batch * seq, D_HEADS * D_DH), BF16),
        grid_spec=grid_spec,
        compiler_params=_cparams(("arbitrary", "arbitrary")),
        name="swa_prompt",
    )(sinks, q, kv, kv)


def _swa_step_kernel(q_ref, kvn_ref, bk_ref, bv_ref, sink_ref, o_ref, nk_ref, nv_ref):
    w = bk_ref.shape[1]
    rep = D_HEADS // D_KV
    q = q_ref[0]
    dup = (_iota((D_DH, LANES), 1) % D_DH == _iota((D_DH, LANES), 0)).astype(BF16)
    q2 = _dot(q, dup)
    own = (_iota((D_HEADS, LANES), 1) // D_DH) == (_iota((D_HEADS, LANES), 0) // rep)
    qm = jnp.where(own, q2, 0.0)
    bk = bk_ref[0]
    bv = bv_ref[0]
    k_new = kvn_ref[0, :, :LANES]
    v_new = kvn_ref[0, :, LANES:]
    s = _dot_nt(qm.astype(BF16), bk.astype(BF16)) * (D_DH ** -0.5)
    s_new = jnp.sum(qm * k_new.astype(BF16).astype(F32), axis=1, keepdims=True) * (D_DH ** -0.5)
    sink = sink_ref[...]
    m = jnp.maximum(jnp.maximum(jnp.max(s, axis=1, keepdims=True), s_new), sink)
    e = jnp.exp(s - m)
    e_new = jnp.exp(s_new - m)
    den = jnp.sum(e, axis=1, keepdims=True) + e_new + jnp.exp(sink - m)
    o = _dot((e / den).astype(BF16), bv.astype(BF16))
    o = o + (e_new / den).astype(BF16).astype(F32) * v_new.astype(BF16).astype(F32)
    o = jnp.where(own, o, 0.0)
    o = o + pltpu.roll(o, D_DH, 1)
    o_ref[0] = o[:, :D_DH].astype(o_ref.dtype)
    nk_ref[0, 0:w - 1, :] = bk_ref[0, 1:w, :]
    nk_ref[0, w - 1:w, :] = k_new
    nv_ref[0, 0:w - 1, :] = bv_ref[0, 1:w, :]
    nv_ref[0, w - 1:w, :] = v_new


def _swa_step(q, kv_new, buf_k, buf_v, sinks):
    bd = q.shape[0]
    w = buf_k.shape[1]
    buf = pl.BlockSpec((1, w, LANES), lambda b: (b, 0, 0))
    return pl.pallas_call(
        _swa_step_kernel,
        out_shape=[jax.ShapeDtypeStruct((bd, D_HEADS, D_DH), BF16),
                   jax.ShapeDtypeStruct((bd, w, LANES), F32),
                   jax.ShapeDtypeStruct((bd, w, LANES), F32)],
        grid=(bd,),
        in_specs=[pl.BlockSpec((1, D_HEADS, D_DH), lambda b: (b, 0, 0)),
                  pl.BlockSpec((1, 1, 2 * LANES), lambda b: (b, 0, 0)),
                  buf, buf,
                  pl.BlockSpec((D_HEADS, 1), lambda b: (0, 0))],
        out_specs=[pl.BlockSpec((1, D_HEADS, D_DH), lambda b: (b, 0, 0)), buf, buf],
        compiler_params=_cparams(("arbitrary",)),
        name="swa_step",
    )(q.reshape(bd, D_HEADS, D_DH), kv_new.reshape(bd, 1, 2 * LANES),
      buf_k.reshape(bd, w, LANES), buf_v.reshape(bd, w, LANES), sinks.reshape(D_HEADS, 1))


def _even_weights(wt, a_qn, a_kn, a_kin, b_w2):
    o = [0]
    for s in (A_HEADS * A_DH, A_KV * A_DH, A_KV * A_DH, IDX_HEADS * IDX_DIM, IDX_DIM, IDX_HEADS,
              B_HEADS * B_DK, B_HEADS * B_DK, B_HEADS * B_DV, B_RANK, B_HEADS * B_DV):
        o.append(o[-1] + s)
    d = wt.shape[1]
    col = lambda i: wt[o[i]:o[i + 1]]
    pad = LANES - (IDX_DIM + IDX_HEADS + B_RANK)
    return dict(
        vb=col(8).astype(BF16),
        small=jnp.concatenate([col(4), col(5), col(9), jnp.zeros((pad, d), wt.dtype)], axis=0).astype(BF16),
        qkr=jnp.concatenate([col(6), col(7), col(10)], axis=0).astype(BF16),
        qa_gain=jnp.tile(a_qn, A_HEADS),
        ka_gain=jnp.tile(a_kn, A_KV),
        small_gain=jnp.concatenate([a_kin, jnp.zeros((LANES - IDX_DIM,), F32)]),
        w2pad=jnp.zeros((LANES, B_HEADS * B_DK), F32).at[
            IDX_DIM + IDX_HEADS:IDX_DIM + IDX_HEADS + B_RANK].set(b_w2).astype(BF16),
    )


def _odd_weights(wt, c_bi, c_bf, d_qn, d_kn):
    o = [0]
    for s in (C_HEADS * C_DK, C_HEADS * C_DK, C_HEADS * C_DV, C_HEADS, C_HEADS, C_HEADS * C_DV,
              D_HEADS * D_DH, D_KV * D_DH, D_KV * D_DH):
        o.append(o[-1] + s)
    d = wt.shape[1]
    col = lambda i: wt[o[i]:o[i + 1]]
    return dict(
        qko=jnp.concatenate([col(0), col(1), col(5)], axis=0).astype(BF16),
        small=jnp.concatenate([col(3), col(4), jnp.zeros((LANES - 2 * C_HEADS, d), wt.dtype)],
                              axis=0).astype(BF16),
        qd=col(6).astype(BF16),
        kvd=jnp.concatenate([col(7), col(8)], axis=0).astype(BF16),
        qd_gain=jnp.tile(d_qn, D_HEADS),
        kd_gain=jnp.concatenate([jnp.tile(d_kn, D_KV), jnp.ones((LANES,), F32)]),
        bias=jnp.concatenate([c_bi, c_bf, jnp.zeros((LANES - 2 * C_HEADS,), F32)]).reshape(1, LANES),
    )


def _even_in(h, w_all, e, ew):
    n_q, n_kv, n_qi = A_HEADS * A_DH, A_KV * A_DH, IDX_HEADS * IDX_DIM
    q, = _proj(h, w_all, "rms128", (BF16,), ew["qa_gain"], window=(e, 0, n_q))
    k32, k16 = _proj(h, w_all, "rms128", (F32, BF16), ew["ka_gain"], window=(e, n_q, n_kv))
    v32, v16 = _proj(h, w_all, "raw", (F32, BF16), window=(e, n_q + n_kv, n_kv))
    qi, = _proj(h, w_all, "raw", (BF16,), window=(e, n_q + 2 * n_kv, n_qi), scale=IDX_DIM ** -0.5)
    vb, = _proj(h, ew["vb"], "raw", (BF16,))
    small, kia, kib = _proj(h, ew["small"], "small_even", (F32, BF16, BF16), ew["small_gain"])
    qkr, = _proj(h, ew["qkr"], "raw", (F32,))
    return dict(q=q, k32=k32, k16=k16, v32=v32, v16=v16, qi=qi, vb=vb,
                small=small, kia=kia, kib=kib, qkr=qkr)


def _odd_in(h, w_all, o, ow):
    qko, = _proj(h, ow["qko"], "raw", (F32,))
    vc, = _proj(h, w_all, "raw", (BF16,), window=(o, 2 * C_HEADS * C_DK, C_HEADS * C_DV))
    small, = _proj(h, ow["small"], "raw", (F32,))
    qd, = _proj(h, ow["qd"], "rms64", (BF16,), ow["qd_gain"])
    kv32, = _proj(h, ow["kvd"], "kd_vd", (F32,), ow["kd_gain"])
    return dict(qko=qko, vc=vc, small=small, qd=qd, kv32=kv32)


def kernel(x_prompt, x_sample, cache_a_k, cache_a_v, cache_a_kidx, state_b_s, state_c_c, state_c_n,
           state_c_m, cache_d_k, cache_d_v, state_ffn_conv, page_table, norm_mix, norm_ffn,
           even_w_in, even_w_out, a_q_norm, a_k_norm, a_kidx_norm, b_gate_w2, b_gate_b, b_head_norm,
           odd_w_in, odd_w_out, c_i_bias, c_f_bias, c_head_norm, d_q_norm, d_k_norm, d_sinks,
           ffn_w_up, ffn_conv_w, ffn_conv_b, ffn_w_down):
    bp, seq, d_model = x_prompt.shape
    bd = x_sample.shape[0]
    depth = norm_mix.shape[0]
    d_ff = ffn_w_down.shape[1]
    mp = bp * seq
    xp = x_prompt.reshape(mp, d_model)
    xs = x_sample.reshape(bd, d_model)
    hp = _rms_cast(xp, norm_mix[0], tm=512)
    hs = _rms_cast(xs, norm_mix[0], tm=bd)
    pool_ki_t = jnp.swapaxes(cache_a_kidx, 2, 3)
    even_wt = jnp.swapaxes(even_w_in, 1, 2)
    odd_wt = jnp.swapaxes(odd_w_in, 1, 2)

    p_out = {k: [] for k in ("ak", "av", "aki", "bs", "cc", "cn", "cm", "dk", "dv", "conv")}
    s_out = {k: [] for k in p_out}
    for layer in range(depth):
        if layer % 2 == 0:
            e = layer // 2
            ew = _even_weights(even_wt[e], a_q_norm[e], a_k_norm[e], a_kidx_norm[e], b_gate_w2[e])
            w_out = even_w_out[e].astype(BF16)
            z = _even_in(hp, even_wt, e, ew)
            a_o = _dsa_prompt(z["q"], z["k16"], z["v16"], z["qi"], z["kia"], z["kib"], z["small"],
                              batch=bp, seq=seq)
            g_o, s_fin = _gla_prompt(z["qkr"], z["vb"], z["small"], ew["w2pad"], b_gate_b[e],
                                     b_head_norm[e], batch=bp, seq=seq)
            p_out["ak"].append(z["k32"].reshape(bp, seq, A_KV, A_DH))
            p_out["av"].append(z["v32"].reshape(bp, seq, A_KV, A_DH))
            p_out["aki"].append(z["small"][:, :IDX_DIM].reshape(bp, seq, IDX_DIM))
            p_out["bs"].append(s_fin)
            mix_p = (a_o, g_o)
            z = _even_in(hs, even_wt, e, ew)
            small = z["small"]
            idx = _dsa_select(page_table,
                              z["qi"].reshape(bd, IDX_HEADS, IDX_DIM),
                              small[:, IDX_DIM:IDX_DIM + IDX_HEADS].reshape(bd, IDX_HEADS, 1),
                              small[:, :IDX_DIM].reshape(bd, 1, IDX_DIM),
                              pool_ki_t, layer=e)
            a_o = _dsa_gather_attend(idx, z["q"].reshape(bd, A_HEADS, A_DH),
                                     z["k32"].reshape(bd, A_KV, A_DH), z["v32"].reshape(bd, A_KV, A_DH),
                                     cache_a_k, cache_a_v, layer=e)
            g_o, s_new = _gla_step(z["qkr"], z["vb"], small, ew["w2pad"], b_gate_b[e],
                                   b_head_norm[e], state_b_s[e])
            s_out["ak"].append(z["k32"].reshape(bd, 1, A_KV, A_DH))
            s_out["av"].append(z["v32"].reshape(bd, 1, A_KV, A_DH))
            s_out["aki"].append(small[:, :IDX_DIM].reshape(bd, 1, IDX_DIM))
            s_out["bs"].append(s_new)
            mix_s = (a_o.reshape(bd, A_HEADS * A_DH), g_o.reshape(bd, B_HEADS * B_DV))
        else:
            o = layer // 2
            ow = _odd_weights(odd_wt[o], c_i_bias[o], c_f_bias[o], d_q_norm[o], d_k_norm[o])
            w_out = odd_w_out[o].astype(BF16)
            z = _odd_in(hp, odd_wt, o, ow)
            c_h, cf, nf, mf = _mlstm_prompt(z["qko"], z["vc"], z["small"], ow["bias"], c_head_norm[o],
                                            batch=bp, seq=seq)
            d_o = _swa_prompt(z["qd"], z["kv32"], d_sinks[o], batch=bp, seq=seq)
            kv = z["kv32"].reshape(bp, seq, 2, D_KV, D_DH)[:, seq - WINDOW:]
            p_out["cc"].append(cf)
            p_out["cn"].append(nf[:, :, 0, :])
            p_out["cm"].append(mf[:, :, 0, 0])
            p_out["dk"].append(kv[:, :, 0])
            p_out["dv"].append(kv[:, :, 1])
            mix_p = (c_h, d_o)
            z = _odd_in(hs, odd_wt, o, ow)
            c_h, c_new, n_new, m_new = _mlstm_step(z["qko"], z["vc"], z["small"], ow["bias"],
                                                   c_head_norm[o], state_c_c[o], state_c_n[o],
                                                   state_c_m[o])
            d_o, nbk, nbv = _swa_step(z["qd"], z["kv32"], cache_d_k[o], cache_d_v[o], d_sinks[o])
            s_out["cc"].append(c_new)
            s_out["cn"].append(n_new)
            s_out["cm"].append(m_new.reshape(bd, C_HEADS))
            s_out["dk"].append(nbk.reshape(bd, WINDOW, D_KV, D_DH))
            s_out["dv"].append(nbv.reshape(bd, WINDOW, D_KV, D_DH))
            mix_s = (c_h.reshape(bd, C_HEADS * C_DV), d_o.reshape(bd, D_HEADS * D_DH))

        xp, hfp = _outproj(mix_p[0], mix_p[1], w_out, xp, norm_ffn[layer])
        xs, hfs = _outproj(mix_s[0], mix_s[1], w_out, xs, norm_ffn[layer])

        xp, glp = _ffn(xp, hfp, ffn_w_up, ffn_w_down, layer, ffn_conv_w[layer], ffn_conv_b[layer],
                       jnp.zeros((bp, SUBLANES, d_ff), F32), seq_len=seq, stepwise=False)
        xs, gls = _ffn(xs, hfs, ffn_w_up, ffn_w_down, layer, ffn_conv_w[layer], ffn_conv_b[layer],
                       state_ffn_conv[layer].swapaxes(0, 1), seq_len=1, stepwise=True)
        if layer + 1 < depth:
            hp = _rms_cast(xp, norm_mix[layer + 1], tm=512)
            hs = _rms_cast(xs, norm_mix[layer + 1], tm=bd)
        tiles = glp.shape[0] // bp
        p_out["conv"].append(glp.reshape(bp, tiles, SUBLANES, d_ff)[:, -1, SUBLANES - (CONV_W - 1):])
        s_out["conv"].append(jnp.stack([state_ffn_conv[layer][:, 1], gls], axis=1))

    order = ("ak", "av", "aki", "bs", "cc", "cn", "cm", "dk", "dv", "conv")
    return (xp.reshape(bp, seq, d_model), xs.reshape(bd, 1, d_model),
            *[jnp.stack(p_out[k]) for k in order],
            *[jnp.stack(s_out[k]) for k in order])
```
